```python
import jax, jax.numpy as jnp
from jax import lax
import numpy as np

D_MODEL = 1024
BATCH = 32
SEQ = 2048
DEPTH = 1

RET_HEADS = 4
RET_QK_DIM = 128
RET_V_DIM = 256
RET_CHUNK = 128
MOBA_HEADS = 8
MOBA_HEAD_DIM = 64
MOBA_BLOCK = 256
MOBA_TOPK = 3
MOBA_Q_CHUNK = 16
D_FF = 4 * D_MODEL
NORM_EPS = 1e-6
MASK_VALUE = -1e30

RET_QK_W = RET_HEADS * RET_QK_DIM
RET_V_W = RET_HEADS * RET_V_DIM
MOBA_W = MOBA_HEADS * MOBA_HEAD_DIM
IN_SIZES = [RET_QK_W, RET_QK_W, RET_V_W, RET_V_W, MOBA_W, MOBA_W, MOBA_W, D_MODEL, D_MODEL]
IN_WIDTH = int(sum(IN_SIZES))
SPLIT_IDX = [int(v) for v in np.cumsum(IN_SIZES)[:-1]]

kernel_name = "hybrid_retention_moba_gated_block"


def rms_norm(x, g):
    xf = x.astype(jnp.float32)
    y = xf * lax.rsqrt(jnp.mean(xf * xf, axis=-1, keepdims=True) + NORM_EPS)
    return (y * g.astype(jnp.float32)).astype(x.dtype)


def group_rms(x):
    xf = x.astype(jnp.float32)
    return (xf * lax.rsqrt(jnp.mean(xf * xf, axis=-1, keepdims=True) + NORM_EPS)).astype(x.dtype)


def to_heads(z, n_heads):
    b, t, w = z.shape
    return z.reshape(b, t, n_heads, w // n_heads).transpose(0, 2, 1, 3)


def from_heads(z):
    b, h, t, d = z.shape
    return z.transpose(0, 2, 1, 3).reshape(b, t, h * d)


def retention(q, k, v):
    B, H, T, dk = q.shape
    dv = v.shape[-1]
    C = RET_CHUNK
    N = T // C
    dt = q.dtype
    log_g = jnp.log1p(-jnp.exp2(-5.0 - jnp.arange(H, dtype=jnp.float32)))
    pos = jnp.arange(C, dtype=jnp.float32)
    diff = pos[:, None] - pos[None, :]
    inner_decay = jnp.where(diff >= 0, jnp.exp(log_g[:, None, None] * jnp.maximum(diff, 0.0)), 0.0).astype(dt)
    k_decay = jnp.exp(log_g[:, None] * (C - 1.0 - pos)).astype(dt)
    q_decay = jnp.exp(log_g[:, None] * (pos + 1.0)).astype(dt)
    chunk_decay = jnp.exp(log_g * C).astype(dt)[:, None, None]

    qc = q.reshape(B, H, N, C, dk)
    kc = k.reshape(B, H, N, C, dk) * (dk ** -0.5)
    vc = v.reshape(B, H, N, C, dv)
    scores = jnp.einsum('bhnid,bhnjd->bhnij', qc, kc) * inner_decay[None, :, None]
    inner = jnp.einsum('bhnij,bhnje->bhnie', scores, vc)
    chunk_kv = jnp.einsum('bhnjd,bhnje->nbhde', kc * k_decay[None, :, None, :, None], vc)

    def step(S, kv):
        return S * chunk_decay + kv, S

    _, S_prev = lax.scan(step, jnp.zeros_like(chunk_kv[0]), chunk_kv)
    cross = jnp.einsum('bhnid,nbhde->bhnie', qc * q_decay[None, :, None, :, None], S_prev)
    return (inner + cross).reshape(B, H, T, dv)


def moba_attention(q, k, v):
    B, H, T, d = q.shape
    L = MOBA_BLOCK
    nb = -(-T // L)
    Tp = nb * L
    padw = ((0, 0), (0, 0), (0, Tp - T), (0, 0))
    q = jnp.pad(q, padw)
    k = jnp.pad(k, padw)
    v = jnp.pad(v, padw)
    kb = k.reshape(B, H, nb, L, d)
    vb = v.reshape(B, H, nb, L, d)
    scale = d ** -0.5

    k_mean = jnp.mean(kb.astype(jnp.float32), axis=3)
    gate = jnp.einsum('bhtd,bhnd->bhtn', q.astype(jnp.float32), k_mean)
    q_blk = jnp.arange(Tp) // L
    past = jnp.arange(nb)[None, :] < q_blk[:, None]
    gate = jnp.where(past, gate, -jnp.inf)
    topk = min(MOBA_TOPK, nb)
    g_val, g_idx = lax.top_k(gate, topk)
    sel_valid = jnp.isfinite(g_val)

    slopes = jnp.exp2(-8.0 / H * jnp.arange(1, H + 1, dtype=jnp.float32))
    bi = jnp.arange(B)[:, None, None, None]
    hi = jnp.arange(H)[None, :, None, None]
    offs = jnp.arange(L)
    QC = MOBA_Q_CHUNK

    def chunk(c):
        start = c * QC
        qc = lax.dynamic_slice_in_dim(q, start, QC, axis=2)
        idx = lax.dynamic_slice_in_dim(g_idx, start, QC, axis=2)
        ok = lax.dynamic_slice_in_dim(sel_valid, start, QC, axis=2)
        t_idx = start + jnp.arange(QC)
        ob = start // L
        k_own = lax.dynamic_index_in_dim(kb, ob, axis=2, keepdims=False)
        v_own = lax.dynamic_index_in_dim(vb, ob, axis=2, keepdims=False)
        k_sel = kb[bi, hi, idx]
        v_sel = vb[bi, hi, idx]
        s_own = ob * L + offs
        s_sel = idx[..., None] * L + offs
        d_own = (t_idx[:, None] - s_own[None, :]).astype(jnp.float32)
        d_sel = (t_idx[:, None, None] - s_sel).astype(jnp.float32)
        l_own = jnp.einsum('bhqd,bhsd->bhqs', qc, k_own).astype(jnp.float32) * scale - slopes[:, None, None] * d_own
        l_own = jnp.where(s_own[None, :] <= t_idx[:, None], l_own, MASK_VALUE)
        l_sel = jnp.einsum('bhqd,bhqnsd->bhqns', qc, k_sel).astype(jnp.float32) * scale - slopes[:, None, None, None] * d_sel
        l_sel = jnp.where(ok[..., None], l_sel, MASK_VALUE)
        logits = jnp.concatenate([l_own, l_sel.reshape(B, H, QC, topk * L)], axis=-1)
        p = jax.nn.softmax(logits, axis=-1).astype(v.dtype)
        p_sel = p[..., L:].reshape(B, H, QC, topk, L)
        return (jnp.einsum('bhqs,bhsd->bhqd', p[..., :L], v_own)
                + jnp.einsum('bhqns,bhqnsd->bhqd', p_sel, v_sel))

    out = lax.map(chunk, jnp.arange(Tp // QC))
    out = jnp.moveaxis(out, 0, 2).reshape(B, H, Tp, d)
    return out[:, :, :T]


def setup_inputs(seed: int = 0) -> dict:
    key = jax.random.key(seed)
    ks = jax.random.split(key, 12)
    f32 = jnp.float32

    def w(k, shape, fan_in):
        return jax.random.normal(k, shape, f32) * (fan_in ** -0.5)

    return {
        "x": jax.random.normal(ks[0], (BATCH, SEQ, D_MODEL), f32),
        "norm1_g": 1.0 + 0.02 * jax.random.normal(ks[1], (DEPTH, D_MODEL), f32),
        "w_in": w(ks[2], (DEPTH, D_MODEL, IN_WIDTH), D_MODEL),
        "b_gate": 0.1 * jax.random.normal(ks[3], (DEPTH, 2 * D_MODEL), f32),
        "w_ret_o": w(ks[4], (DEPTH, RET_V_W, D_MODEL), RET_V_W),
        "w_moba_o": w(ks[5], (DEPTH, MOBA_W, D_MODEL), MOBA_W),
        "w_out": w(ks[6], (DEPTH, D_MODEL, D_MODEL), D_MODEL),
        "norm2_g": 1.0 + 0.02 * jax.random.normal(ks[7], (DEPTH, D_MODEL), f32),
        "w_up": w(ks[8], (DEPTH, D_MODEL, D_FF), D_MODEL),
        "w_down": w(ks[9], (DEPTH, D_FF, D_MODEL), D_FF),
        "normf_g": 1.0 + 0.02 * jax.random.normal(ks[10], (D_MODEL,), f32),
    }


def reference(x, norm1_g, w_in, b_gate, w_ret_o, w_moba_o, w_out, norm2_g, w_up, w_down, normf_g):
    for layer in range(DEPTH):
        h = rms_norm(x, norm1_g[layer])
        z = h @ w_in[layer]
        rq, rk, rv, rg, mq, mk, mv, gr, gm = jnp.split(z, SPLIT_IDX, axis=-1)
        ret = retention(to_heads(rq, RET_HEADS), to_heads(rk, RET_HEADS), to_heads(rv, RET_HEADS))
        ret = from_heads(group_rms(ret)) * jax.nn.silu(rg)
        y_ret = ret @ w_ret_o[layer]
        mo = moba_attention(to_heads(mq, MOBA_HEADS), to_heads(mk, MOBA_HEADS), to_heads(mv, MOBA_HEADS))
        y_moba = from_heads(mo) @ w_moba_o[layer]
        bg_r, bg_m = jnp.split(b_gate[layer], 2)
        merged = jax.nn.sigmoid(gr + bg_r) * y_ret + jax.nn.sigmoid(gm + bg_m) * y_moba
        x = x + merged @ w_out[layer]
        h2 = rms_norm(x, norm2_g[layer])
        x = x + jnp.square(jax.nn.relu(h2 @ w_up[layer])) @ w_down[layer]
    return rms_norm(x, normf_g)
```

```python
import functools

import numpy as np
import jax
import jax.numpy as jnp
from jax import lax
from jax.experimental import pallas as pl
from jax.experimental.pallas import tpu as pltpu

RET_HEADS = 4
RET_QK_DIM = 128
RET_V_DIM = 256
MOBA_HEADS = 8
MOBA_HEAD_DIM = 64
MOBA_BLOCK = 256
MOBA_TOPK = 3
NORM_EPS = 1e-6
MASK_VALUE = -1e30

RET_QK_W = RET_HEADS * RET_QK_DIM
RET_V_W = RET_HEADS * RET_V_DIM
MOBA_W = MOBA_HEADS * MOBA_HEAD_DIM

RET_KERNEL_CHUNK = 256
LANES = 128
VMEM_LIMIT_BYTES = 56 * 1024 * 1024

F32 = jnp.float32
BF16 = jnp.bfloat16


def _dot(a, b):
    return jnp.dot(a, b, preferred_element_type=F32)


def _dot_nt(a, b):
    return lax.dot_general(a, b, (((1,), (1,)), ((), ())), preferred_element_type=F32)


def _rms(x):
    return x * lax.rsqrt(jnp.mean(x * x, axis=-1, keepdims=True) + NORM_EPS)


def _sigmoid(x):
    return 1.0 / (1.0 + jnp.exp(-x))


def _in_proj_kernel(x_ref, g_ref, w_ref, z_ref, h_ref):
    @pl.when(pl.program_id(1) == 0)
    def _():
        h_ref[...] = (_rms(x_ref[...]) * g_ref[...]).astype(BF16)

    z_ref[...] = _dot(h_ref[...], w_ref[...]).astype(z_ref.dtype)


def _in_proj(x2d, g, w_bf16, *, tm, tn):
    m, d = x2d.shape
    n = w_bf16.shape[1]
    assert m % tm == 0 and n % tn == 0
    return pl.pallas_call(
        _in_proj_kernel,
        grid=(m // tm, n // tn),
        in_specs=[
            pl.BlockSpec((tm, d), lambda i, j: (i, 0)),
            pl.BlockSpec((1, d), lambda i, j: (0, 0)),
            pl.BlockSpec((d, tn), lambda i, j: (0, j)),
        ],
        out_specs=pl.BlockSpec((tm, tn), lambda i, j: (i, j)),
        out_shape=jax.ShapeDtypeStruct((m, n), BF16),
        scratch_shapes=[pltpu.VMEM((tm, d), BF16)],
        compiler_params=pltpu.CompilerParams(
            dimension_semantics=("parallel", "arbitrary"),
            vmem_limit_bytes=VMEM_LIMIT_BYTES),
        name="in_proj",
    )(x2d, g, w_bf16)


def _retention_kernel(cd_ref, q_ref, k_ref, v_ref, g_ref, dm_ref, kd_ref, qd_ref, o_ref, *, chunk):
    t = q_ref.shape[1]
    cd = cd_ref[pl.program_id(1)]
    dm = dm_ref[0]
    kd = kd_ref[0]
    qd = qd_ref[0]
    state = jnp.zeros((q_ref.shape[2], v_ref.shape[2]), F32)
    for n in range(t // chunk):
        rows = slice(n * chunk, (n + 1) * chunk)
        qc = q_ref[0, rows, :]
        kc = k_ref[0, rows, :]
        vc = v_ref[0, rows, :]
        scores = (_dot_nt(qc, kc) * dm).astype(BF16)
        q_dec = (qc.astype(F32) * qd).astype(BF16)
        o = _dot(scores, vc) + _dot(q_dec, state.astype(BF16))
        gate = g_ref[0, rows, :].astype(F32)
        o_ref[0, rows, :] = (_rms(o) * (gate * _sigmoid(gate))).astype(o_ref.dtype)
        k_dec_t = (kc.astype(F32) * kd).T.astype(BF16)
        state = state * cd + _dot(k_dec_t, vc)


def _retention_tables(chunk):
    h = np.arange(RET_HEADS, dtype=np.float64)
    log_g = np.log1p(-np.exp2(-5.0 - h))
    pos = np.arange(chunk, dtype=np.float64)
    diff = pos[:, None] - pos[None, :]
    scale = RET_QK_DIM ** -0.5
    dm = np.where(diff >= 0, np.exp(log_g[:, None, None] * np.maximum(diff, 0.0)), 0.0) * scale
    kd = np.exp(log_g[:, None] * (chunk - 1.0 - pos)) * scale
    qd = np.exp(log_g[:, None] * (pos + 1.0))
    cd = np.exp(log_g * chunk)
    as32 = lambda a: jnp.asarray(a.astype(np.float32))
    return as32(cd), as32(dm), as32(kd[:, :, None]), as32(qd[:, :, None])


def _retention(z3d, *, q_blk, k_blk, v_blk, g_blk):
    b, t, _ = z3d.shape
    c = RET_KERNEL_CHUNK
    cd, dm, kd, qd = _retention_tables(c)
    qk_spec = lambda base: pl.BlockSpec((1, t, RET_QK_DIM), lambda i, h: (i, 0, base + h))
    v_spec = lambda base: pl.BlockSpec((1, t, RET_V_DIM), lambda i, h: (i, 0, base + h))
    return pl.pallas_call(
        functools.partial(_retention_kernel, chunk=c),
        grid=(b, RET_HEADS),
        in_specs=[
            pl.BlockSpec(memory_space=pltpu.SMEM),
            qk_spec(q_blk), qk_spec(k_blk), v_spec(v_blk), v_spec(g_blk),
            pl.BlockSpec((1, c, c), lambda i, h: (h, 0, 0)),
            pl.BlockSpec((1, c, 1), lambda i, h: (h, 0, 0)),
            pl.BlockSpec((1, c, 1), lambda i, h: (h, 0, 0)),
        ],
        out_specs=pl.BlockSpec((1, t, RET_V_DIM), lambda i, h: (i, 0, h)),
        out_shape=jax.ShapeDtypeStruct((b, t, RET_V_W), BF16),
        compiler_params=pltpu.CompilerParams(
            dimension_semantics=("parallel", "arbitrary"),
            vmem_limit_bytes=VMEM_LIMIT_BYTES),
        name="retention",
    )(cd, z3d, z3d, z3d, z3d, dm, kd, qd)


def _moba_kernel(slope_ref, q_ref, k_ref, v_ref, o_ref, *, blk, topk, head_dim):
    t = q_ref.shape[1]
    nb = t // blk
    width = q_ref.shape[2]
    heads_per_step = width // head_dim
    scale = head_dim ** -0.5
    q_all = q_ref[0]
    k_all = k_ref[0]
    v_all = v_ref[0]
    lane = lax.broadcasted_iota(jnp.int32, (1, width), 1)

    k_mean = jnp.mean(k_all.astype(F32).reshape(nb, blk, width), axis=1)

    row = lax.broadcasted_iota(jnp.int32, (blk, blk), 0)
    col = lax.broadcasted_iota(jnp.int32, (blk, blk), 1)
    rel = (row - col).astype(F32)
    causal = col <= row

    g_row = lax.broadcasted_iota(jnp.int32, (nb, t), 0)
    g_qblk = lax.broadcasted_iota(jnp.int32, (nb, t), 1) // blk
    eligible = g_row < g_qblk

    outs = []
    for head in range(heads_per_step):
        in_head = (lane >= head * head_dim) & (lane < (head + 1) * head_dim)
        slope = slope_ref[pl.program_id(1) * heads_per_step + head]
        q_h = jnp.where(in_head, q_all, jnp.zeros_like(q_all))

        km_h = jnp.where(in_head, k_mean, 0.0)
        gate = lax.dot_general(km_h, q_h.astype(F32), (((1,), (1,)), ((), ())),
                               precision=lax.Precision.HIGHEST, preferred_element_type=F32)
        gate = jnp.where(eligible, gate, -jnp.inf)
        rank = jnp.zeros((nb, t), jnp.int32)
        for m in range(nb):
            g_m = gate[m:m + 1, :]
            beats = (g_m > gate) | ((g_m == gate) & (m < g_row))
            rank = rank + beats.astype(jnp.int32)
        selected = eligible & (rank < topk) & (jnp.abs(gate) < jnp.inf)
        sel_bias_t = jnp.where(selected, 0.0, MASK_VALUE)
        sel_bias_t = jnp.concatenate(
            [sel_bias_t, jnp.full((LANES - nb, t), MASK_VALUE, F32)], axis=0)
        sel_bias = sel_bias_t.T

        alibi = (-slope) * rel
        own_bias = jnp.where(causal, alibi, MASK_VALUE)
        q_s = (q_h.astype(F32) * scale).astype(BF16)

        head_out = []
        for qb in range(nb):
            rows = slice(qb * blk, (qb + 1) * blk)
            keys = slice(0, (qb + 1) * blk)
            s = _dot_nt(q_s[rows], k_all[keys])
            sb = sel_bias[rows]
            parts = []
            for n in range(qb):
                colb = sb[:, n:n + 1] - slope * float((qb - n) * blk)
                parts.append(s[:, n * blk:(n + 1) * blk] + (alibi + colb))
            parts.append(s[:, qb * blk:] + own_bias)
            logits = jnp.concatenate(parts, axis=1) if qb else parts[0]
            m_row = jnp.max(logits, axis=-1, keepdims=True)
            p = jnp.exp(logits - m_row)
            denom = jnp.sum(p, axis=-1, keepdims=True)
            pv = _dot(p.astype(BF16), v_all[keys])
            head_out.append(pv / denom)
        outs.append((in_head, jnp.concatenate(head_out, axis=0)))

    result = outs[0][1]
    for in_head, val in outs[1:]:
        result = jnp.where(in_head, val, result)
    o_ref[0] = result.astype(o_ref.dtype)


def _moba(z3d, *, q_blk, k_blk, v_blk):
    b, t, _ = z3d.shape
    assert t % MOBA_BLOCK == 0
    width = LANES
    steps = MOBA_W // width
    slopes = jnp.asarray(np.exp2(-8.0 / MOBA_HEADS * np.arange(1, MOBA_HEADS + 1)).astype(np.float32))
    spec = lambda base: pl.BlockSpec((1, t, width), lambda i, h: (i, 0, base + h))
    return pl.pallas_call(
        functools.partial(_moba_kernel, blk=MOBA_BLOCK, topk=MOBA_TOPK, head_dim=MOBA_HEAD_DIM),
        grid=(b, steps),
        in_specs=[pl.BlockSpec(memory_space=pltpu.SMEM), spec(q_blk), spec(k_blk), spec(v_blk)],
        out_specs=pl.BlockSpec((1, t, width), lambda i, h: (i, 0, h)),
        out_shape=jax.ShapeDtypeStruct((b, t, MOBA_W), BF16),
        compiler_params=pltpu.CompilerParams(
            dimension_semantics=("parallel", "arbitrary"),
            vmem_limit_bytes=VMEM_LIMIT_BYTES),
        name="moba",
    )(slopes, z3d, z3d, z3d)


def _post_kernel(x_ref, ret_ref, mo_ref, gr_ref, gm_ref, bgr_ref, bgm_ref, wro_ref, wmo_ref,
                 wout_ref, g2_ref, wup_ref, wdn_ref, gf_ref, o_ref, *, ff_chunk):
    y_ret = _dot(ret_ref[...], wro_ref[...])
    y_moba = _dot(mo_ref[...], wmo_ref[...])
    gate_r = _sigmoid(gr_ref[...].astype(F32) + bgr_ref[...])
    gate_m = _sigmoid(gm_ref[...].astype(F32) + bgm_ref[...])
    merged = (gate_r * y_ret + gate_m * y_moba).astype(BF16)
    x1 = x_ref[...] + _dot(merged, wout_ref[...])
    h2 = (_rms(x1) * g2_ref[...]).astype(BF16)
    acc = x1
    for c in range(0, wup_ref.shape[1], ff_chunk):
        u = _dot(h2, wup_ref[:, c:c + ff_chunk])
        u = jnp.square(jnp.maximum(u, 0.0)).astype(BF16)
        acc = acc + _dot(u, wdn_ref[c:c + ff_chunk, :])
    o_ref[...] = (_rms(acc) * gf_ref[...]).astype(o_ref.dtype)


def _post(x2d, ret2d, mo2d, z2d, bg_r, bg_m, w_ret_o, w_moba_o, w_out, g2, w_up, w_down, gf,
          *, tm, gr_blk, gm_blk):
    m, d = x2d.shape
    dff = w_up.shape[1]
    row = lambda width: pl.BlockSpec((tm, width), lambda i: (i, 0))
    const = lambda shape: pl.BlockSpec(shape, lambda i: (0, 0), pipeline_mode=pl.Buffered(1))
    return pl.pallas_call(
        functools.partial(_post_kernel, ff_chunk=1024),
        grid=(m // tm,),
        in_specs=[
            row(d), row(RET_V_W), row(MOBA_W),
            pl.BlockSpec((tm, d), lambda i: (i, gr_blk)),
            pl.BlockSpec((tm, d), lambda i: (i, gm_blk)),
            const((1, d)), const((1, d)),
            const((RET_V_W, d)), const((MOBA_W, d)), const((d, d)),
            const((1, d)), const((d, dff)), const((dff, d)), const((1, d)),
        ],
        out_specs=row(d),
        out_shape=jax.ShapeDtypeStruct((m, d), x2d.dtype),
        compiler_params=pltpu.CompilerParams(
            dimension_semantics=("parallel",),
            vmem_limit_bytes=VMEM_LIMIT_BYTES),
        name="merge_mlp",
    )(x2d, ret2d, mo2d, z2d, z2d, bg_r, bg_m, w_ret_o, w_moba_o, w_out, g2, w_up, w_down, gf)


def _permuted_in_columns(d):
    sizes = [RET_QK_W, RET_QK_W, RET_V_W, RET_V_W, MOBA_W, MOBA_W, MOBA_W, d, d]
    starts = np.concatenate([[0], np.cumsum(sizes)[:-1]])
    order = [7, 8, 2, 3, 0, 1, 4, 5, 6]
    perm = np.concatenate([np.arange(starts[i], starts[i] + sizes[i]) for i in order])
    new_start, acc = {}, 0
    for i in order:
        new_start[i] = acc
        acc += sizes[i]
    return perm, new_start


def kernel(x, norm1_g, w_in, b_gate, w_ret_o, w_moba_o, w_out, norm2_g, w_up, w_down, normf_g):
    b, t, d = x.shape
    assert norm1_g.shape[0] == 1, "single-layer block"
    m = b * t
    perm, start = _permuted_in_columns(d)
    w_in_p = w_in[0][:, perm].astype(BF16)

    x2d = x.reshape(m, d)
    z2d = _in_proj(x2d, norm1_g, w_in_p, tm=1024, tn=512)
    z3d = z2d.reshape(b, t, -1)

    ret = _retention(z3d, q_blk=start[0] // RET_QK_DIM, k_blk=start[1] // RET_QK_DIM,
                     v_blk=start[2] // RET_V_DIM, g_blk=start[3] // RET_V_DIM)
    mo = _moba(z3d, q_blk=start[4] // LANES, k_blk=start[5] // LANES, v_blk=start[6] // LANES)

    bg = b_gate[0].reshape(2, 1, d)
    out = _post(x2d, ret.reshape(m, RET_V_W), mo.reshape(m, MOBA_W), z2d, bg[0], bg[1],
                w_ret_o[0].astype(BF16), w_moba_o[0].astype(BF16), w_out[0].astype(BF16),
                norm2_g, w_up[0].astype(BF16), w_down[0].astype(BF16), normf_g.reshape(1, d),
                tm=512, gr_blk=start[7] // d, gm_blk=start[8] // d)
    return out.reshape(b, t, d)
```

```python
import functools

import numpy as np
import jax
import jax.numpy as jnp
from jax import lax
from jax.experimental import pallas as pl
from jax.experimental.pallas import tpu as pltpu

RET_HEADS = 4
RET_QK_DIM = 128
RET_V_DIM = 256
MOBA_HEADS = 8
MOBA_HEAD_DIM = 64
MOBA_BLOCK = 256
MOBA_TOPK = 3
NORM_EPS = 1e-6
MASK_VALUE = -1e30

RET_QK_W = RET_HEADS * RET_QK_DIM
RET_V_W = RET_HEADS * RET_V_DIM
MOBA_W = MOBA_HEADS * MOBA_HEAD_DIM

RET_KERNEL_CHUNK = 256
LANES = 128
MOBA_STEPS = MOBA_W // LANES
PROJ_CHUNK = 512
ONES_ROWS = 16
VMEM_LIMIT_BYTES = 56 * 1024 * 1024

F32 = jnp.float32
BF16 = jnp.bfloat16


def _dot(a, b):
    return jnp.dot(a, b, preferred_element_type=F32)


def _dot_nt(a, b):
    return lax.dot_general(a, b, (((1,), (1,)), ((), ())), preferred_element_type=F32)


def _rms(x):
    return x * lax.rsqrt(jnp.mean(x * x, axis=-1, keepdims=True) + NORM_EPS)


def _sigmoid(x):
    return 1.0 / (1.0 + jnp.exp(-x))


def _in_proj_layout(d):
    return (("ret_q", RET_HEADS, RET_QK_DIM), ("ret_k", RET_HEADS, RET_QK_DIM),
            ("ret_v", RET_HEADS, RET_V_DIM), ("ret_g", RET_HEADS, RET_V_DIM),
            ("moba_q", MOBA_STEPS, LANES), ("moba_k", MOBA_STEPS, LANES), ("moba_v", MOBA_STEPS, LANES),
            ("gate_ret", 1, d), ("gate_moba", 1, d))


def _in_proj_kernel(x_ref, g_ref, w_ref, *out_refs, layout):
    h = (_rms(x_ref[...]) * g_ref[...]).astype(BF16)
    off = 0
    for ref, (_, heads, width) in zip(out_refs, layout):
        for c in range(0, heads * width, PROJ_CHUNK):
            zc = _dot(h, w_ref[:, off + c:off + c + PROJ_CHUNK]).astype(BF16)
            if heads == 1:
                ref[:, c:c + PROJ_CHUNK] = zc
            else:
                for j in range(PROJ_CHUNK // width):
                    ref[0, c // width + j] = zc[:, j * width:(j + 1) * width]
        off += heads * width


def _in_proj(x2d, g, w_bf16, *, b, t, tm):
    m, d = x2d.shape
    layout = _in_proj_layout(d)
    assert t % tm == 0 and w_bf16.shape[1] == sum(h * w for _, h, w in layout)
    tiles = t // tm
    out_specs, out_shapes = [], []
    for _, heads, width in layout:
        assert (heads * width) % PROJ_CHUNK == 0 and PROJ_CHUNK % width == 0 or heads == 1
        if heads == 1:
            out_specs.append(pl.BlockSpec((tm, width), lambda i: (i, 0)))
            out_shapes.append(jax.ShapeDtypeStruct((m, width), BF16))
        else:
            out_specs.append(pl.BlockSpec((1, heads, tm, width), lambda i: (i // tiles, 0, i % tiles, 0)))
            out_shapes.append(jax.ShapeDtypeStruct((b, heads, t, width), BF16))
    return pl.pallas_call(
        functools.partial(_in_proj_kernel, layout=layout),
        grid=(m // tm,),
        in_specs=[
            pl.BlockSpec((tm, d), lambda i: (i, 0)),
            pl.BlockSpec((1, d), lambda i: (0, 0)),
            pl.BlockSpec(w_bf16.shape, lambda i: (0, 0), pipeline_mode=pl.Buffered(1)),
        ],
        out_specs=out_specs,
        out_shape=out_shapes,
        compiler_params=pltpu.CompilerParams(
            dimension_semantics=("parallel",),
            vmem_limit_bytes=VMEM_LIMIT_BYTES),
        name="in_proj",
    )(x2d, g, w_bf16)


def _retention_kernel(cd_ref, q_ref, k_ref, v_ref, g_ref, dm_ref, kd_ref, qd_ref, o_ref, *, chunk):
    t = q_ref.shape[2]
    cd = cd_ref[pl.program_id(1)]
    dm = dm_ref[0]
    kd = kd_ref[0]
    qd = qd_ref[0]
    state = jnp.zeros((q_ref.shape[3], v_ref.shape[3]), F32)
    for n in range(t // chunk):
        rows = slice(n * chunk, (n + 1) * chunk)
        qc = q_ref[0, 0, rows, :]
        kc = k_ref[0, 0, rows, :]
        vc = v_ref[0, 0, rows, :]
        scores = (_dot_nt(qc, kc) * dm).astype(BF16)
        q_dec = (qc.astype(F32) * qd).astype(BF16)
        o = _dot(scores, vc) + _dot(q_dec, state.astype(BF16))
        gate = g_ref[0, 0, rows, :].astype(F32)
        o_ref[0, rows, :] = (_rms(o) * (gate * _sigmoid(gate))).astype(o_ref.dtype)
        k_dec_t = (kc.astype(F32) * kd).T.astype(BF16)
        state = state * cd + _dot(k_dec_t, vc)


def _retention_tables(chunk):
    h = np.arange(RET_HEADS, dtype=np.float64)
    log_g = np.log1p(-np.exp2(-5.0 - h))
    pos = np.arange(chunk, dtype=np.float64)
    diff = pos[:, None] - pos[None, :]
    scale = RET_QK_DIM ** -0.5
    dm = np.where(diff >= 0, np.exp(log_g[:, None, None] * np.maximum(diff, 0.0)), 0.0) * scale
    kd = np.exp(log_g[:, None] * (chunk - 1.0 - pos)) * scale
    qd = np.exp(log_g[:, None] * (pos + 1.0))
    cd = np.exp(log_g * chunk)
    as32 = lambda a: jnp.asarray(a.astype(np.float32))
    return as32(cd), as32(dm), as32(kd[:, :, None]), as32(qd[:, :, None])


def _retention(q, k, v, g):
    b, heads, t, _ = q.shape
    c = RET_KERNEL_CHUNK
    cd, dm, kd, qd = _retention_tables(c)
    head_spec = lambda width: pl.BlockSpec((1, 1, t, width), lambda i, h: (i, h, 0, 0))
    return pl.pallas_call(
        functools.partial(_retention_kernel, chunk=c),
        grid=(b, heads),
        in_specs=[
            pl.BlockSpec(memory_space=pltpu.SMEM),
            head_spec(RET_QK_DIM), head_spec(RET_QK_DIM), head_spec(RET_V_DIM), head_spec(RET_V_DIM),
            pl.BlockSpec((1, c, c), lambda i, h: (h, 0, 0)),
            pl.BlockSpec((1, c, 1), lambda i, h: (h, 0, 0)),
            pl.BlockSpec((1, c, 1), lambda i, h: (h, 0, 0)),
        ],
        out_specs=pl.BlockSpec((1, t, RET_V_DIM), lambda i, h: (i, 0, h)),
        out_shape=jax.ShapeDtypeStruct((b, t, RET_V_W), BF16),
        compiler_params=pltpu.CompilerParams(
            dimension_semantics=("parallel", "arbitrary"),
            vmem_limit_bytes=VMEM_LIMIT_BYTES),
        name="retention",
    )(cd, q, k, v, g, dm, kd, qd)


def _moba_key_features(t, blk):
    s = np.arange(t)
    feats = np.zeros((t, LANES), np.float32)
    feats[s, s // blk] = 1.0
    nb = t // blk
    feats[:, nb] = s % blk
    feats[:, nb + 1] = s // blk
    return jnp.asarray(feats, dtype=BF16)


def _moba_kernel(slope_ref, q_ref, k_ref, v_ref, kx_ref, o_ref, *, blk, topk, head_dim):
    t = q_ref.shape[2]
    nb = t // blk
    width = q_ref.shape[3]
    heads_per_step = width // head_dim
    scale = head_dim ** -0.5
    q_all = q_ref[0, 0]
    k_all = k_ref[0, 0]
    k_aug = jnp.concatenate([k_all, kx_ref[...]], axis=1)
    v_t = v_ref[0, 0].astype(F32).T.astype(BF16)
    lane = lax.broadcasted_iota(jnp.int32, (1, width), 1)

    k_mean = jnp.mean(k_all.astype(F32).reshape(nb, blk, width), axis=1)

    key_i = lax.broadcasted_iota(jnp.int32, (blk, blk), 0)
    qry_i = lax.broadcasted_iota(jnp.int32, (blk, blk), 1)
    causal_t = key_i <= qry_i

    g_row = lax.broadcasted_iota(jnp.int32, (nb, t), 0)
    g_qblk = lax.broadcasted_iota(jnp.int32, (nb, t), 1) // blk
    eligible = g_row < g_qblk
    own = g_row == g_qblk
    x_row = lax.broadcasted_iota(jnp.int32, (LANES - nb, t), 0)

    ones_rows = jnp.ones((ONES_ROWS, t), BF16)
    q_augs, v_exts = [], []
    for head in range(heads_per_step):
        in_head = (lane >= head * head_dim) & (lane < (head + 1) * head_dim)
        slope = slope_ref[pl.program_id(1) * heads_per_step + head]
        q_h = jnp.where(in_head, q_all, jnp.zeros_like(q_all))

        km_h = jnp.where(in_head, k_mean, 0.0)
        gate = lax.dot_general(km_h, q_h.astype(F32), (((1,), (1,)), ((), ())),
                               precision=lax.Precision.HIGHEST, preferred_element_type=F32)
        gate = jnp.where(eligible, gate, -jnp.inf)
        rank = jnp.zeros((nb, t), jnp.int32)
        for m in range(nb):
            g_m = gate[m:m + 1, :]
            beats = (g_m > gate) | ((g_m == gate) & (m < g_row))
            rank = rank + beats.astype(jnp.int32)
        selected = eligible & (rank < topk) & (jnp.abs(gate) < jnp.inf)
        sel_bias_t = jnp.where(selected | own, 0.0, MASK_VALUE)
        extra_t = jnp.where(x_row == 0, slope, jnp.where(x_row == 1, slope * float(blk), 0.0))
        q_extra = jnp.concatenate([sel_bias_t, extra_t], axis=0).T
        q_s = (q_h.astype(F32) * scale).astype(BF16)
        q_augs.append(jnp.concatenate([q_s, q_extra.astype(BF16)], axis=1))
        v_exts.append(jnp.concatenate([v_t[head * head_dim:(head + 1) * head_dim, :], ones_rows], axis=0))

    def scores(qb, head):
        nk = (qb + 1) * blk
        return _dot_nt(k_aug[:nk], q_augs[head][qb * blk:nk])

    def attend(qb, head, s_t):
        past = qb * blk
        diag = jnp.where(causal_t, s_t[past:], MASK_VALUE)
        m_col = jnp.max(diag, axis=0, keepdims=True)
        if qb:
            m_col = jnp.maximum(m_col, jnp.max(s_t[:past], axis=0, keepdims=True))
        v_e = v_exts[head]
        acc = _dot(v_e[:, past:past + blk], jnp.exp((diag - m_col).astype(BF16)))
        if qb:
            acc = acc + _dot(v_e[:, :past], jnp.exp((s_t[:past] - m_col).astype(BF16)))
        return acc[:head_dim] / acc[head_dim:head_dim + 1]

    order = [(qb, head) for qb in range(nb) for head in range(heads_per_step)]
    outs = [[None] * nb for _ in range(heads_per_step)]
    pending = scores(*order[0])
    for i, (qb, head) in enumerate(order):
        nxt = scores(*order[i + 1]) if i + 1 < len(order) else None
        outs[head][qb] = attend(qb, head, pending)
        pending = nxt
    head_out_t = [jnp.concatenate(o, axis=1) for o in outs]

    o_ref[0] = jnp.concatenate(head_out_t, axis=0).T.astype(o_ref.dtype)


def _moba(q, k, v):
    b, steps, t, width = q.shape
    assert t % MOBA_BLOCK == 0 and t // MOBA_BLOCK + 2 <= LANES
    slopes = jnp.asarray(np.exp2(-8.0 / MOBA_HEADS * np.arange(1, MOBA_HEADS + 1)).astype(np.float32))
    spec = pl.BlockSpec((1, 1, t, width), lambda i, h: (i, h, 0, 0))
    return pl.pallas_call(
        functools.partial(_moba_kernel, blk=MOBA_BLOCK, topk=MOBA_TOPK, head_dim=MOBA_HEAD_DIM),
        grid=(b, steps),
        in_specs=[pl.BlockSpec(memory_space=pltpu.SMEM), spec, spec, spec,
                  pl.BlockSpec((t, LANES), lambda i, h: (0, 0))],
        out_specs=pl.BlockSpec((1, t, width), lambda i, h: (i, 0, h)),
        out_shape=jax.ShapeDtypeStruct((b, t, MOBA_W), BF16),
        compiler_params=pltpu.CompilerParams(
            dimension_semantics=("parallel", "arbitrary"),
            vmem_limit_bytes=VMEM_LIMIT_BYTES),
        name="moba",
    )(slopes, q, k, v, _moba_key_features(t, MOBA_BLOCK))


def _post_kernel(x_ref, ret_ref, mo_ref, gr_ref, gm_ref, bgr_ref, bgm_ref, wro_ref, wmo_ref,
                 wout_ref, g2_ref, wup_ref, wdn_ref, gf_ref, o_ref, *, ff_chunk):
    y_ret = _dot(ret_ref[...], wro_ref[...])
    y_moba = _dot(mo_ref[...], wmo_ref[...])
    gate_r = _sigmoid(gr_ref[...].astype(F32) + bgr_ref[...])
    gate_m = _sigmoid(gm_ref[...].astype(F32) + bgm_ref[...])
    merged = (gate_r * y_ret + gate_m * y_moba).astype(BF16)
    x1 = x_ref[...] + _dot(merged, wout_ref[...])
    h2 = (_rms(x1) * g2_ref[...]).astype(BF16)
    acc = x1
    for c in range(0, wup_ref.shape[1], ff_chunk):
        u = _dot(h2, wup_ref[:, c:c + ff_chunk])
        u = jnp.square(jnp.maximum(u, 0.0)).astype(BF16)
        acc = acc + _dot(u, wdn_ref[c:c + ff_chunk, :])
    o_ref[...] = (_rms(acc) * gf_ref[...]).astype(o_ref.dtype)


def _post(x2d, ret2d, mo2d, gr2d, gm2d, bg_r, bg_m, w_ret_o, w_moba_o, w_out, g2, w_up, w_down, gf,
          *, tm):
    m, d = x2d.shape
    dff = w_up.shape[1]
    row = lambda width: pl.BlockSpec((tm, width), lambda i: (i, 0))
    const = lambda shape: pl.BlockSpec(shape, lambda i: (0, 0), pipeline_mode=pl.Buffered(1))
    return pl.pallas_call(
        functools.partial(_post_kernel, ff_chunk=1024),
        grid=(m // tm,),
        in_specs=[
            row(d), row(RET_V_W), row(MOBA_W), row(d), row(d),
            const((1, d)), const((1, d)),
            const((RET_V_W, d)), const((MOBA_W, d)), const((d, d)),
            const((1, d)), const((d, dff)), const((dff, d)), const((1, d)),
        ],
        out_specs=row(d),
        out_shape=jax.ShapeDtypeStruct((m, d), x2d.dtype),
        compiler_params=pltpu.CompilerParams(
            dimension_semantics=("parallel",),
            vmem_limit_bytes=VMEM_LIMIT_BYTES),
        name="merge_mlp",
    )(x2d, ret2d, mo2d, gr2d, gm2d, bg_r, bg_m, w_ret_o, w_moba_o, w_out, g2, w_up, w_down, gf)


def kernel(x, norm1_g, w_in, b_gate, w_ret_o, w_moba_o, w_out, norm2_g, w_up, w_down, normf_g):
    b, t, d = x.shape
    assert norm1_g.shape[0] == 1, "single-layer block"
    m = b * t
    x2d = x.reshape(m, d)
    rq, rk, rv, rg, mq, mk, mv, gr, gm = _in_proj(x2d, norm1_g, w_in[0].astype(BF16), b=b, t=t, tm=512)
    ret = _retention(rq, rk, rv, rg)
    mo = _moba(mq, mk, mv)
    bg = b_gate[0].reshape(2, 1, d)
    out = _post(x2d, ret.reshape(m, RET_V_W), mo.reshape(m, MOBA_W), gr, gm, bg[0], bg[1],
                w_ret_o[0].astype(BF16), w_moba_o[0].astype(BF16), w_out[0].astype(BF16),
                norm2_g, w_up[0].astype(BF16), w_down[0].astype(BF16), normf_g.reshape(1, d),
                tm=512)
    return out.reshape(b, t, d)
```

```python
import functools

import numpy as np
import jax
import jax.numpy as jnp
from jax import lax
from jax.experimental import pallas as pl
from jax.experimental.pallas import tpu as pltpu

RET_HEADS = 4
RET_QK_DIM = 128
RET_V_DIM = 256
MOBA_HEADS = 8
MOBA_HEAD_DIM = 64
MOBA_BLOCK = 256
MOBA_TOPK = 3
NORM_EPS = 1e-6
MASK_VALUE = -1e30

RET_QK_W = RET_HEADS * RET_QK_DIM
RET_V_W = RET_HEADS * RET_V_DIM
MOBA_W = MOBA_HEADS * MOBA_HEAD_DIM

RET_KERNEL_CHUNK = 256
LANES = 128
MOBA_STEPS = MOBA_W // LANES
MOBA_STEP_HEADS = LANES // MOBA_HEAD_DIM
MOBA_HEAD_FEATS = 16
PROJ_CHUNK = 512
ONES_ROWS = 16
VMEM_LIMIT_BYTES = 56 * 1024 * 1024

F32 = jnp.float32
BF16 = jnp.bfloat16


def _dot(a, b):
    return jnp.dot(a, b, preferred_element_type=F32)


def _dot_nt(a, b):
    return lax.dot_general(a, b, (((1,), (1,)), ((), ())), preferred_element_type=F32)


def _rms(x):
    return x * lax.rsqrt(jnp.mean(x * x, axis=-1, keepdims=True) + NORM_EPS)


def _sigmoid(x):
    return 1.0 / (1.0 + jnp.exp(-x))


def _in_proj_layout(d):
    return (("ret_q", RET_HEADS, RET_QK_DIM), ("ret_k", RET_HEADS, RET_QK_DIM),
            ("ret_v", RET_HEADS, RET_V_DIM), ("ret_g", RET_HEADS, RET_V_DIM),
            ("moba_q", MOBA_STEPS, LANES), ("moba_k", MOBA_STEPS, LANES), ("moba_v", MOBA_STEPS, LANES),
            ("gate_ret", 1, d), ("gate_moba", 1, d))


def _in_proj_kernel(x_ref, g_ref, w_ref, *out_refs, layout):
    h = (_rms(x_ref[...]) * g_ref[...]).astype(BF16)
    off = 0
    for ref, (_, heads, width) in zip(out_refs, layout):
        for c in range(0, heads * width, PROJ_CHUNK):
            zc = _dot(h, w_ref[:, off + c:off + c + PROJ_CHUNK]).astype(BF16)
            if heads == 1:
                ref[:, c:c + PROJ_CHUNK] = zc
            else:
                for j in range(PROJ_CHUNK // width):
                    ref[0, c // width + j] = zc[:, j * width:(j + 1) * width]
        off += heads * width


def _in_proj(x2d, g, w_bf16, *, b, t, tm):
    m, d = x2d.shape
    layout = _in_proj_layout(d)
    assert t % tm == 0 and w_bf16.shape[1] == sum(h * w for _, h, w in layout)
    tiles = t // tm
    out_specs, out_shapes = [], []
    for _, heads, width in layout:
        assert (heads * width) % PROJ_CHUNK == 0 and PROJ_CHUNK % width == 0 or heads == 1
        if heads == 1:
            out_specs.append(pl.BlockSpec((tm, width), lambda i: (i, 0)))
            out_shapes.append(jax.ShapeDtypeStruct((m, width), BF16))
        else:
            out_specs.append(pl.BlockSpec((1, heads, tm, width), lambda i: (i // tiles, 0, i % tiles, 0)))
            out_shapes.append(jax.ShapeDtypeStruct((b, heads, t, width), BF16))
    return pl.pallas_call(
        functools.partial(_in_proj_kernel, layout=layout),
        grid=(m // tm,),
        in_specs=[
            pl.BlockSpec((tm, d), lambda i: (i, 0)),
            pl.BlockSpec((1, d), lambda i: (0, 0)),
            pl.BlockSpec(w_bf16.shape, lambda i: (0, 0), pipeline_mode=pl.Buffered(1)),
        ],
        out_specs=out_specs,
        out_shape=out_shapes,
        compiler_params=pltpu.CompilerParams(
            dimension_semantics=("parallel",),
            vmem_limit_bytes=VMEM_LIMIT_BYTES),
        name="in_proj",
    )(x2d, g, w_bf16)


def _retention_kernel(cd_ref, q_ref, k_ref, v_ref, g_ref, dm_ref, kd_ref, qd_ref, o_ref, *, chunk):
    t = q_ref.shape[2]
    cd = cd_ref[pl.program_id(1)]
    dm = dm_ref[0]
    kd = kd_ref[0]
    qd = qd_ref[0]
    state = jnp.zeros((q_ref.shape[3], v_ref.shape[3]), F32)
    for n in range(t // chunk):
        rows = slice(n * chunk, (n + 1) * chunk)
        qc = q_ref[0, 0, rows, :]
        kc = k_ref[0, 0, rows, :]
        vc = v_ref[0, 0, rows, :]
        scores = (_dot_nt(qc, kc) * dm).astype(BF16)
        q_dec = (qc.astype(F32) * qd).astype(BF16)
        o = _dot(scores, vc) + _dot(q_dec, state.astype(BF16))
        gate = g_ref[0, 0, rows, :].astype(F32)
        o_ref[0, rows, :] = (_rms(o) * (gate * _sigmoid(gate))).astype(o_ref.dtype)
        k_dec_t = (kc.astype(F32) * kd).T.astype(BF16)
        state = state * cd + _dot(k_dec_t, vc)


def _retention_tables(chunk):
    h = np.arange(RET_HEADS, dtype=np.float64)
    log_g = np.log1p(-np.exp2(-5.0 - h))
    pos = np.arange(chunk, dtype=np.float64)
    diff = pos[:, None] - pos[None, :]
    scale = RET_QK_DIM ** -0.5
    dm = np.where(diff >= 0, np.exp(log_g[:, None, None] * np.maximum(diff, 0.0)), 0.0) * scale
    kd = np.exp(log_g[:, None] * (chunk - 1.0 - pos)) * scale
    qd = np.exp(log_g[:, None] * (pos + 1.0))
    cd = np.exp(log_g * chunk)
    as32 = lambda a: jnp.asarray(a.astype(np.float32))
    return as32(cd), as32(dm), as32(kd[:, :, None]), as32(qd[:, :, None])


def _retention(q, k, v, g):
    b, heads, t, _ = q.shape
    c = RET_KERNEL_CHUNK
    cd, dm, kd, qd = _retention_tables(c)
    head_spec = lambda width: pl.BlockSpec((1, 1, t, width), lambda i, h: (i, h, 0, 0))
    return pl.pallas_call(
        functools.partial(_retention_kernel, chunk=c),
        grid=(b, heads),
        in_specs=[
            pl.BlockSpec(memory_space=pltpu.SMEM),
            head_spec(RET_QK_DIM), head_spec(RET_QK_DIM), head_spec(RET_V_DIM), head_spec(RET_V_DIM),
            pl.BlockSpec((1, c, c), lambda i, h: (h, 0, 0)),
            pl.BlockSpec((1, c, 1), lambda i, h: (h, 0, 0)),
            pl.BlockSpec((1, c, 1), lambda i, h: (h, 0, 0)),
        ],
        out_specs=pl.BlockSpec((1, t, RET_V_DIM), lambda i, h: (i, 0, h)),
        out_shape=jax.ShapeDtypeStruct((b, t, RET_V_W), BF16),
        compiler_params=pltpu.CompilerParams(
            dimension_semantics=("parallel", "arbitrary"),
            vmem_limit_bytes=VMEM_LIMIT_BYTES),
        name="retention",
    )(cd, q, k, v, g, dm, kd, qd)


def _moba_key_features(t, blk):
    s = np.arange(t)
    feats = np.zeros((t, LANES), np.float32)
    nb = t // blk
    for f0 in range(0, MOBA_STEP_HEADS * MOBA_HEAD_FEATS, MOBA_HEAD_FEATS):
        feats[s, f0 + s // blk] = 1.0
        feats[:, f0 + nb] = s % blk
        feats[:, f0 + nb + 1] = s // blk
    return jnp.asarray(feats, dtype=BF16)


def _moba_kernel(slope_ref, q_ref, k_ref, v_ref, kx_ref, o_ref, *, blk, topk, head_dim):
    t = q_ref.shape[2]
    nb = t // blk
    width = q_ref.shape[3]
    heads_per_step = width // head_dim
    scale = head_dim ** -0.5
    q_all = q_ref[0, 0]
    k_all = k_ref[0, 0]
    k_aug = jnp.concatenate([k_all, kx_ref[...]], axis=1)
    v_t = v_ref[0, 0].astype(F32).T.astype(BF16)
    lane = lax.broadcasted_iota(jnp.int32, (1, width), 1)
    lane2 = lax.broadcasted_iota(jnp.int32, (1, 2 * width), 1)
    in_head = [(lane >= h * head_dim) & (lane < (h + 1) * head_dim) for h in range(heads_per_step)]

    k_mean = jnp.mean(k_all.astype(F32).reshape(nb, blk, width), axis=1)
    km_hi = k_mean.astype(BF16).astype(F32)
    km_mid = (k_mean - km_hi).astype(BF16).astype(F32)
    km_lo = k_mean - km_hi - km_mid
    pieces = [jnp.where(in_head[h], part, 0.0) for h in range(heads_per_step) for part in (km_hi, km_mid, km_lo)]
    gates = _dot_nt(jnp.concatenate(pieces, axis=0).astype(BF16), q_all)

    key_i = lax.broadcasted_iota(jnp.int32, (blk, blk), 0)
    qry_i = lax.broadcasted_iota(jnp.int32, (blk, blk), 1)
    causal_t = key_i <= qry_i

    g_row = lax.broadcasted_iota(jnp.int32, (nb, t), 0)
    g_qblk = lax.broadcasted_iota(jnp.int32, (nb, t), 1) // blk
    eligible = g_row < g_qblk
    own = g_row == g_qblk
    x_row = lax.broadcasted_iota(jnp.int32, (MOBA_HEAD_FEATS - nb, t), 0)

    extra_rows = []
    for head in range(heads_per_step):
        slope = slope_ref[pl.program_id(1) * heads_per_step + head]
        g0 = 3 * head * nb
        gate = gates[g0:g0 + nb] + gates[g0 + nb:g0 + 2 * nb] + gates[g0 + 2 * nb:g0 + 3 * nb]
        gate = jnp.where(eligible, gate, -jnp.inf)
        rank = jnp.zeros((nb, t), jnp.int32)
        for m in range(nb):
            g_m = gate[m:m + 1, :]
            beats = (g_m > gate) | ((g_m == gate) & (m < g_row))
            rank = rank + beats.astype(jnp.int32)
        selected = eligible & (rank < topk) & (jnp.abs(gate) < jnp.inf)
        extra_rows.append(jnp.where(selected | own, 0.0, MASK_VALUE))
        extra_rows.append(jnp.where(x_row == 0, slope, jnp.where(x_row == 1, slope * float(blk), 0.0)))
    extra_rows.append(jnp.zeros((LANES - heads_per_step * MOBA_HEAD_FEATS, t), F32))
    q_extra = jnp.concatenate(extra_rows, axis=0).T.astype(BF16)
    q_both = jnp.concatenate([q_all * scale, q_extra], axis=1)

    ones_rows = jnp.ones((ONES_ROWS, t), BF16)
    q_augs, v_exts = [], []
    for head in range(heads_per_step):
        f0 = width + head * MOBA_HEAD_FEATS
        mine = ((lane2 >= head * head_dim) & (lane2 < (head + 1) * head_dim)) | (
            (lane2 >= f0) & (lane2 < f0 + MOBA_HEAD_FEATS))
        q_augs.append(jnp.where(mine, q_both, jnp.zeros_like(q_both)))
        v_exts.append(jnp.concatenate([v_t[head * head_dim:(head + 1) * head_dim, :], ones_rows], axis=0))

    def scores(qb, head):
        nk = (qb + 1) * blk
        return _dot_nt(k_aug[:nk], q_augs[head][qb * blk:nk])

    def attend(qb, head, s_t):
        past = qb * blk
        diag = jnp.where(causal_t, s_t[past:], MASK_VALUE)
        m_col = jnp.max(diag, axis=0, keepdims=True)
        if qb:
            m_col = jnp.maximum(m_col, jnp.max(s_t[:past], axis=0, keepdims=True))
        v_e = v_exts[head]
        acc = _dot(v_e[:, past:past + blk], jnp.exp((diag - m_col).astype(BF16)))
        if qb:
            acc = acc + _dot(v_e[:, :past], jnp.exp((s_t[:past] - m_col).astype(BF16)))
        return acc[:head_dim] / acc[head_dim:head_dim + 1]

    heads = range(heads_per_step)
    order = list(reversed(range(nb)))
    pending = [scores(order[0], h) for h in heads]
    for i, qb in enumerate(order):
        nxt = [scores(order[i + 1], h) for h in heads] if i + 1 < nb else None
        block_out_t = [attend(qb, h, pending[h]) for h in heads]
        o_ref[0, qb * blk:(qb + 1) * blk, :] = jnp.concatenate(block_out_t, axis=0).T.astype(o_ref.dtype)
        pending = nxt


def _moba(q, k, v):
    b, steps, t, width = q.shape
    assert t % MOBA_BLOCK == 0 and t // MOBA_BLOCK + 2 <= MOBA_HEAD_FEATS
    slopes = jnp.asarray(np.exp2(-8.0 / MOBA_HEADS * np.arange(1, MOBA_HEADS + 1)).astype(np.float32))
    spec = pl.BlockSpec((1, 1, t, width), lambda i, h: (i, h, 0, 0))
    return pl.pallas_call(
        functools.partial(_moba_kernel, blk=MOBA_BLOCK, topk=MOBA_TOPK, head_dim=MOBA_HEAD_DIM),
        grid=(b, steps),
        in_specs=[pl.BlockSpec(memory_space=pltpu.SMEM), spec, spec, spec,
                  pl.BlockSpec((t, LANES), lambda i, h: (0, 0))],
        out_specs=pl.BlockSpec((1, t, width), lambda i, h: (i, 0, h)),
        out_shape=jax.ShapeDtypeStruct((b, t, MOBA_W), BF16),
        compiler_params=pltpu.CompilerParams(
            dimension_semantics=("parallel", "arbitrary"),
            vmem_limit_bytes=VMEM_LIMIT_BYTES),
        name="moba",
    )(slopes, q, k, v, _moba_key_features(t, MOBA_BLOCK))


def _post_kernel(x_ref, ret_ref, mo_ref, gr_ref, gm_ref, bgr_ref, bgm_ref, wro_ref, wmo_ref,
                 wout_ref, g2_ref, wup_ref, wdn_ref, gf_ref, o_ref, *, ff_chunk):
    y_ret = _dot(ret_ref[...], wro_ref[...])
    y_moba = _dot(mo_ref[...], wmo_ref[...])
    gate_r = _sigmoid(gr_ref[...].astype(F32) + bgr_ref[...])
    gate_m = _sigmoid(gm_ref[...].astype(F32) + bgm_ref[...])
    merged = (gate_r * y_ret + gate_m * y_moba).astype(BF16)
    x1 = x_ref[...] + _dot(merged, wout_ref[...])
    h2 = (_rms(x1) * g2_ref[...]).astype(BF16)
    acc = x1
    for c in range(0, wup_ref.shape[1], ff_chunk):
        u = _dot(h2, wup_ref[:, c:c + ff_chunk])
        u = jnp.square(jnp.maximum(u, 0.0)).astype(BF16)
        acc = acc + _dot(u, wdn_ref[c:c + ff_chunk, :])
    o_ref[...] = (_rms(acc) * gf_ref[...]).astype(o_ref.dtype)


def _post(x2d, ret2d, mo2d, gr2d, gm2d, bg_r, bg_m, w_ret_o, w_moba_o, w_out, g2, w_up, w_down, gf,
          *, tm):
    m, d = x2d.shape
    dff = w_up.shape[1]
    row = lambda width: pl.BlockSpec((tm, width), lambda i: (i, 0))
    const = lambda shape: pl.BlockSpec(shape, lambda i: (0, 0), pipeline_mode=pl.Buffered(1))
    return pl.pallas_call(
        functools.partial(_post_kernel, ff_chunk=1024),
        grid=(m // tm,),
        in_specs=[
            row(d), row(RET_V_W), row(MOBA_W), row(d), row(d),
            const((1, d)), const((1, d)),
            const((RET_V_W, d)), const((MOBA_W, d)), const((d, d)),
            const((1, d)), const((d, dff)), const((dff, d)), const((1, d)),
        ],
        out_specs=row(d),
        out_shape=jax.ShapeDtypeStruct((m, d), x2d.dtype),
        compiler_params=pltpu.CompilerParams(
            dimension_semantics=("parallel",),
            vmem_limit_bytes=VMEM_LIMIT_BYTES),
        name="merge_mlp",
    )(x2d, ret2d, mo2d, gr2d, gm2d, bg_r, bg_m, w_ret_o, w_moba_o, w_out, g2, w_up, w_down, gf)


def kernel(x, norm1_g, w_in, b_gate, w_ret_o, w_moba_o, w_out, norm2_g, w_up, w_down, normf_g):
    b, t, d = x.shape
    assert norm1_g.shape[0] == 1, "single-layer block"
    m = b * t
    x2d = x.reshape(m, d)
    rq, rk, rv, rg, mq, mk, mv, gr, gm = _in_proj(x2d, norm1_g, w_in[0].astype(BF16), b=b, t=t, tm=512)
    ret = _retention(rq, rk, rv, rg)
    mo = _moba(mq, mk, mv)
    bg = b_gate[0].reshape(2, 1, d)
    out = _post(x2d, ret.reshape(m, RET_V_W), mo.reshape(m, MOBA_W), gr, gm, bg[0], bg[1],
                w_ret_o[0].astype(BF16), w_moba_o[0].astype(BF16), w_out[0].astype(BF16),
                norm2_g, w_up[0].astype(BF16), w_down[0].astype(BF16), normf_g.reshape(1, d),
                tm=512)
    return out.reshape(b, t, d)
```

```python
import functools

import numpy as np
import jax
import jax.numpy as jnp
from jax import lax
from jax.experimental import pallas as pl
from jax.experimental.pallas import tpu as pltpu

RET_HEADS = 4
RET_QK_DIM = 128
RET_V_DIM = 256
MOBA_HEADS = 8
MOBA_HEAD_DIM = 64
MOBA_BLOCK = 256
MOBA_TOPK = 3
NORM_EPS = 1e-6
MASK_VALUE = -1e30

RET_QK_W = RET_HEADS * RET_QK_DIM
RET_V_W = RET_HEADS * RET_V_DIM
MOBA_W = MOBA_HEADS * MOBA_HEAD_DIM

RET_KERNEL_CHUNK = 256
RET_STEP_HEADS = 4
LANES = 128
MOBA_STEPS = MOBA_W // LANES
MOBA_STEP_HEADS = LANES // MOBA_HEAD_DIM
MOBA_HEAD_FEATS = 16
PROJ_CHUNK = 512
ONES_ROWS = 16
VMEM_LIMIT_BYTES = 56 * 1024 * 1024

F32 = jnp.float32
BF16 = jnp.bfloat16


def _dot(a, b):
    return jnp.dot(a, b, preferred_element_type=F32)


def _dot_nt(a, b):
    return lax.dot_general(a, b, (((1,), (1,)), ((), ())), preferred_element_type=F32)


def _rms(x):
    return x * lax.rsqrt(jnp.mean(x * x, axis=-1, keepdims=True) + NORM_EPS)


def _sigmoid(x):
    return 1.0 / (1.0 + jnp.exp(-x))


def _in_proj_layout(d):
    return (("ret_q", RET_HEADS, RET_QK_DIM), ("ret_k", RET_HEADS, RET_QK_DIM),
            ("ret_v", RET_HEADS, RET_V_DIM), ("ret_g", RET_HEADS, RET_V_DIM),
            ("moba_q", MOBA_STEPS, LANES), ("moba_k", MOBA_STEPS, LANES), ("moba_v", MOBA_STEPS, LANES),
            ("gate_ret", 1, d), ("gate_moba", 1, d))


def _in_proj_kernel(x_ref, g_ref, w_ref, dq_ref, dk_ref, *out_refs, layout):
    h = (_rms(x_ref[...]) * g_ref[...]).astype(BF16)
    epilogue = {
        "ret_q": lambda z, c: z * dq_ref[:, c:c + PROJ_CHUNK],
        "ret_k": lambda z, c: z * dk_ref[:, c:c + PROJ_CHUNK],
    }
    off = 0
    for ref, (name, heads, width) in zip(out_refs, layout):
        for c in range(0, heads * width, PROJ_CHUNK):
            zc = _dot(h, w_ref[:, off + c:off + c + PROJ_CHUNK])
            if name in epilogue:
                zc = epilogue[name](zc, c)
            zc = zc.astype(BF16)
            if heads == 1:
                ref[:, c:c + PROJ_CHUNK] = zc
            else:
                for j in range(PROJ_CHUNK // width):
                    ref[0, c // width + j] = zc[:, j * width:(j + 1) * width]
        off += heads * width


def _retention_decay_tables(rows, chunk):
    log_g = np.log1p(-np.exp2(-5.0 - np.arange(RET_HEADS, dtype=np.float64)))
    steps = (np.arange(rows) % chunk + 1.0)[:, None, None] * log_g[None, :, None]
    expand = lambda a: jnp.asarray(np.broadcast_to(a, (rows, RET_HEADS, RET_QK_DIM)).reshape(rows, RET_QK_W)
                                   .astype(np.float32))
    return expand(np.exp(steps)), expand(np.exp(-steps) * RET_QK_DIM ** -0.5)


def _in_proj(x2d, g, w_bf16, *, b, t, tm):
    m, d = x2d.shape
    layout = _in_proj_layout(d)
    assert t % tm == 0 and tm % RET_KERNEL_CHUNK == 0
    assert w_bf16.shape[1] == sum(h * w for _, h, w in layout)
    tiles = t // tm
    dq, dk = _retention_decay_tables(tm, RET_KERNEL_CHUNK)
    out_specs, out_shapes = [], []
    for _, heads, width in layout:
        assert (heads * width) % PROJ_CHUNK == 0 and PROJ_CHUNK % width == 0 or heads == 1
        if heads == 1:
            out_specs.append(pl.BlockSpec((tm, width), lambda i: (i, 0)))
            out_shapes.append(jax.ShapeDtypeStruct((m, width), BF16))
        else:
            out_specs.append(pl.BlockSpec((1, heads, tm, width), lambda i: (i // tiles, 0, i % tiles, 0)))
            out_shapes.append(jax.ShapeDtypeStruct((b, heads, t, width), BF16))
    const = lambda a: pl.BlockSpec(a.shape, lambda i: (0, 0), pipeline_mode=pl.Buffered(1))
    return pl.pallas_call(
        functools.partial(_in_proj_kernel, layout=layout),
        grid=(m // tm,),
        in_specs=[
            pl.BlockSpec((tm, d), lambda i: (i, 0)),
            pl.BlockSpec((1, d), lambda i: (0, 0)),
            const(w_bf16), const(dq), const(dk),
        ],
        out_specs=out_specs,
        out_shape=out_shapes,
        compiler_params=pltpu.CompilerParams(
            dimension_semantics=("parallel",),
            vmem_limit_bytes=VMEM_LIMIT_BYTES),
        name="in_proj",
    )(x2d, g, w_bf16, dq, dk)


def _retention_kernel(cd_ref, q_ref, k_ref, v_ref, g_ref, o_ref, *, chunk):
    heads, t = q_ref.shape[1], q_ref.shape[2]
    dv = v_ref.shape[3]
    row = lax.broadcasted_iota(jnp.int32, (chunk, chunk), 0)
    col = lax.broadcasted_iota(jnp.int32, (chunk, chunk), 1)
    causal = col <= row
    cds = [cd_ref[pl.program_id(1) * heads + h] for h in range(heads)]
    states = [jnp.zeros((q_ref.shape[3], dv), F32) for _ in range(heads)]
    for n in range(t // chunk):
        rows = slice(n * chunk, (n + 1) * chunk)
        for h in range(heads):
            qc = q_ref[0, h, rows, :]
            kc = k_ref[0, h, rows, :]
            vc = v_ref[0, h, rows, :]
            scores = jnp.where(causal, _dot_nt(qc, kc), 0.0).astype(BF16)
            o = _dot(scores, vc) + _dot(qc, states[h].astype(BF16))
            half_g = 0.5 * g_ref[0, h, rows, :].astype(F32)
            swish = half_g * (1.0 + jnp.tanh(half_g))
            o_ref[0, rows, h * dv:(h + 1) * dv] = (_rms(o) * swish).astype(o_ref.dtype)
            kv = lax.dot_general(kc, vc, (((0,), (0,)), ((), ())), preferred_element_type=F32)
            states[h] = cds[h] * (states[h] + kv)


def _retention(q, k, v, g):
    b, heads, t, _ = q.shape
    c = RET_KERNEL_CHUNK
    hs = RET_STEP_HEADS
    log_g = np.log1p(-np.exp2(-5.0 - np.arange(RET_HEADS, dtype=np.float64)))
    cd = jnp.asarray(np.exp(log_g * c).astype(np.float32))
    head_spec = lambda width: pl.BlockSpec((1, hs, t, width), lambda i, h: (i, h, 0, 0))
    return pl.pallas_call(
        functools.partial(_retention_kernel, chunk=c),
        grid=(b, heads // hs),
        in_specs=[
            pl.BlockSpec(memory_space=pltpu.SMEM),
            head_spec(RET_QK_DIM), head_spec(RET_QK_DIM), head_spec(RET_V_DIM), head_spec(RET_V_DIM),
        ],
        out_specs=pl.BlockSpec((1, t, hs * RET_V_DIM), lambda i, h: (i, 0, h)),
        out_shape=jax.ShapeDtypeStruct((b, t, RET_V_W), BF16),
        compiler_params=pltpu.CompilerParams(
            dimension_semantics=("parallel", "arbitrary"),
            vmem_limit_bytes=VMEM_LIMIT_BYTES),
        name="retention",
    )(cd, q, k, v, g)


def _moba_key_features(t, blk):
    s = np.arange(t)
    feats = np.zeros((t, LANES), np.float32)
    nb = t // blk
    for f0 in range(0, MOBA_STEP_HEADS * MOBA_HEAD_FEATS, MOBA_HEAD_FEATS):
        feats[s, f0 + s // blk] = 1.0
        feats[:, f0 + nb] = s % blk
        feats[:, f0 + nb + 1] = s // blk
    return jnp.asarray(feats, dtype=BF16)


def _moba_kernel(slope_ref, q_ref, k_ref, v_ref, kx_ref, o_ref, *, blk, topk, head_dim):
    t = q_ref.shape[2]
    nb = t // blk
    width = q_ref.shape[3]
    heads_per_step = width // head_dim
    scale = head_dim ** -0.5
    q_all = q_ref[0, 0]
    k_all = k_ref[0, 0]
    k_aug = jnp.concatenate([k_all, kx_ref[...]], axis=1)
    v_t = v_ref[0, 0].astype(F32).T.astype(BF16)
    lane = lax.broadcasted_iota(jnp.int32, (1, width), 1)
    lane2 = lax.broadcasted_iota(jnp.int32, (1, 2 * width), 1)
    in_head = [(lane >= h * head_dim) & (lane < (h + 1) * head_dim) for h in range(heads_per_step)]

    k_mean = jnp.mean(k_all.astype(F32).reshape(nb, blk, width), axis=1)
    km_hi = k_mean.astype(BF16).astype(F32)
    km_mid = (k_mean - km_hi).astype(BF16).astype(F32)
    km_lo = k_mean - km_hi - km_mid
    pieces = [jnp.where(in_head[h], part, 0.0) for h in range(heads_per_step) for part in (km_hi, km_mid, km_lo)]
    gates = _dot_nt(jnp.concatenate(pieces, axis=0).astype(BF16), q_all)

    key_i = lax.broadcasted_iota(jnp.int32, (blk, blk), 0)
    qry_i = lax.broadcasted_iota(jnp.int32, (blk, blk), 1)
    causal_t = key_i <= qry_i

    g_row = lax.broadcasted_iota(jnp.int32, (nb, t), 0)
    g_qblk = lax.broadcasted_iota(jnp.int32, (nb, t), 1) // blk
    eligible = g_row < g_qblk
    own = g_row == g_qblk
    x_row = lax.broadcasted_iota(jnp.int32, (MOBA_HEAD_FEATS - nb, t), 0)

    extra_rows = []
    for head in range(heads_per_step):
        slope = slope_ref[pl.program_id(1) * heads_per_step + head]
        g0 = 3 * head * nb
        gate = gates[g0:g0 + nb] + gates[g0 + nb:g0 + 2 * nb] + gates[g0 + 2 * nb:g0 + 3 * nb]
        gate = jnp.where(eligible, gate, -jnp.inf)
        rank = jnp.zeros((nb, t), jnp.int32)
        for m in range(nb):
            g_m = gate[m:m + 1, :]
            beats = (g_m > gate) | ((g_m == gate) & (m < g_row))
            rank = rank + beats.astype(jnp.int32)
        selected = eligible & (rank < topk) & (jnp.abs(gate) < jnp.inf)
        extra_rows.append(jnp.where(selected | own, 0.0, MASK_VALUE))
        extra_rows.append(jnp.where(x_row == 0, slope, jnp.where(x_row == 1, slope * float(blk), 0.0)))
    extra_rows.append(jnp.zeros((LANES - heads_per_step * MOBA_HEAD_FEATS, t), F32))
    q_extra = jnp.concatenate(extra_rows, axis=0).T.astype(BF16)
    q_both = jnp.concatenate([q_all * scale, q_extra], axis=1)

    ones_rows = jnp.ones((ONES_ROWS, t), BF16)
    q_augs, v_exts = [], []
    for head in range(heads_per_step):
        f0 = width + head * MOBA_HEAD_FEATS
        mine = ((lane2 >= head * head_dim) & (lane2 < (head + 1) * head_dim)) | (
            (lane2 >= f0) & (lane2 < f0 + MOBA_HEAD_FEATS))
        q_augs.append(jnp.where(mine, q_both, jnp.zeros_like(q_both)))
        v_exts.append(jnp.concatenate([v_t[head * head_dim:(head + 1) * head_dim, :], ones_rows], axis=0))

    def scores(qb, head):
        nk = (qb + 1) * blk
        return _dot_nt(k_aug[:nk], q_augs[head][qb * blk:nk])

    def attend(qb, head, s_t):
        past = qb * blk
        diag = jnp.where(causal_t, s_t[past:], MASK_VALUE)
        m_col = jnp.max(diag, axis=0, keepdims=True)
        if qb:
            m_col = jnp.maximum(m_col, jnp.max(s_t[:past], axis=0, keepdims=True))
        v_e = v_exts[head]
        acc = _dot(v_e[:, past:past + blk], jnp.exp((diag - m_col).astype(BF16)))
        if qb:
            acc = acc + _dot(v_e[:, :past], jnp.exp((s_t[:past] - m_col).astype(BF16)))
        return acc[:head_dim] / acc[head_dim:head_dim + 1]

    heads = range(heads_per_step)
    order = list(reversed(range(nb)))
    pending = [scores(order[0], h) for h in heads]
    for i, qb in enumerate(order):
        nxt = [scores(order[i + 1], h) for h in heads] if i + 1 < nb else None
        block_out_t = [attend(qb, h, pending[h]) for h in heads]
        o_ref[0, qb * blk:(qb + 1) * blk, :] = jnp.concatenate(block_out_t, axis=0).T.astype(o_ref.dtype)
        pending = nxt


def _moba(q, k, v):
    b, steps, t, width = q.shape
    assert t % MOBA_BLOCK == 0 and t // MOBA_BLOCK + 2 <= MOBA_HEAD_FEATS
    slopes = jnp.asarray(np.exp2(-8.0 / MOBA_HEADS * np.arange(1, MOBA_HEADS + 1)).astype(np.float32))
    spec = pl.BlockSpec((1, 1, t, width), lambda i, h: (i, h, 0, 0))
    return pl.pallas_call(
        functools.partial(_moba_kernel, blk=MOBA_BLOCK, topk=MOBA_TOPK, head_dim=MOBA_HEAD_DIM),
        grid=(b, steps),
        in_specs=[pl.BlockSpec(memory_space=pltpu.SMEM), spec, spec, spec,
                  pl.BlockSpec((t, LANES), lambda i, h: (0, 0))],
        out_specs=pl.BlockSpec((1, t, width), lambda i, h: (i, 0, h)),
        out_shape=jax.ShapeDtypeStruct((b, t, MOBA_W), BF16),
        compiler_params=pltpu.CompilerParams(
            dimension_semantics=("parallel", "arbitrary"),
            vmem_limit_bytes=VMEM_LIMIT_BYTES),
        name="moba",
    )(slopes, q, k, v, _moba_key_features(t, MOBA_BLOCK))


def _post_kernel(x_ref, ret_ref, mo_ref, gr_ref, gm_ref, bgr_ref, bgm_ref, wro_ref, wmo_ref,
                 wout_ref, g2_ref, wup_ref, wdn_ref, gf_ref, o_ref, *, ff_chunk):
    y_ret = _dot(ret_ref[...], wro_ref[...])
    y_moba = _dot(mo_ref[...], wmo_ref[...])
    gate_r = _sigmoid(gr_ref[...].astype(F32) + bgr_ref[...])
    gate_m = _sigmoid(gm_ref[...].astype(F32) + bgm_ref[...])
    merged = (gate_r * y_ret + gate_m * y_moba).astype(BF16)
    x1 = x_ref[...] + _dot(merged, wout_ref[...])
    h2 = (_rms(x1) * g2_ref[...]).astype(BF16)
    acc = x1
    for c in range(0, wup_ref.shape[1], ff_chunk):
        u = _dot(h2, wup_ref[:, c:c + ff_chunk])
        u = jnp.square(jnp.maximum(u, 0.0)).astype(BF16)
        acc = acc + _dot(u, wdn_ref[c:c + ff_chunk, :])
    o_ref[...] = (_rms(acc) * gf_ref[...]).astype(o_ref.dtype)


def _post(x2d, ret2d, mo2d, gr2d, gm2d, bg_r, bg_m, w_ret_o, w_moba_o, w_out, g2, w_up, w_down, gf,
          *, tm):
    m, d = x2d.shape
    dff = w_up.shape[1]
    row = lambda width: pl.BlockSpec((tm, width), lambda i: (i, 0))
    const = lambda shape: pl.BlockSpec(shape, lambda i: (0, 0), pipeline_mode=pl.Buffered(1))
    return pl.pallas_call(
        functools.partial(_post_kernel, ff_chunk=1024),
        grid=(m // tm,),
        in_specs=[
            row(d), row(RET_V_W), row(MOBA_W), row(d), row(d),
            const((1, d)), const((1, d)),
            const((RET_V_W, d)), const((MOBA_W, d)), const((d, d)),
            const((1, d)), const((d, dff)), const((dff, d)), const((1, d)),
        ],
        out_specs=row(d),
        out_shape=jax.ShapeDtypeStruct((m, d), x2d.dtype),
        compiler_params=pltpu.CompilerParams(
            dimension_semantics=("parallel",),
            vmem_limit_bytes=VMEM_LIMIT_BYTES),
        name="merge_mlp",
    )(x2d, ret2d, mo2d, gr2d, gm2d, bg_r, bg_m, w_ret_o, w_moba_o, w_out, g2, w_up, w_down, gf)


def kernel(x, norm1_g, w_in, b_gate, w_ret_o, w_moba_o, w_out, norm2_g, w_up, w_down, normf_g):
    b, t, d = x.shape
    assert norm1_g.shape[0] == 1, "single-layer block"
    m = b * t
    x2d = x.reshape(m, d)
    rq, rk, rv, rg, mq, mk, mv, gr, gm = _in_proj(x2d, norm1_g, w_in[0].astype(BF16), b=b, t=t, tm=512)
    ret = _retention(rq, rk, rv, rg)
    mo = _moba(mq, mk, mv)
    bg = b_gate[0].reshape(2, 1, d)
    out = _post(x2d, ret.reshape(m, RET_V_W), mo.reshape(m, MOBA_W), gr, gm, bg[0], bg[1],
                w_ret_o[0].astype(BF16), w_moba_o[0].astype(BF16), w_out[0].astype(BF16),
                norm2_g, w_up[0].astype(BF16), w_down[0].astype(BF16), normf_g.reshape(1, d),
                tm=512)
    return out.reshape(b, t, d)
```

```python
import functools

import numpy as np
import jax
import jax.numpy as jnp
from jax import lax
from jax.experimental import pallas as pl
from jax.experimental.pallas import tpu as pltpu

RET_HEADS = 4
RET_QK_DIM = 128
RET_V_DIM = 256
MOBA_HEADS = 8
MOBA_HEAD_DIM = 64
MOBA_BLOCK = 256
MOBA_TOPK = 3
NORM_EPS = 1e-6
MASK_VALUE = -1e30

RET_QK_W = RET_HEADS * RET_QK_DIM
RET_V_W = RET_HEADS * RET_V_DIM
MOBA_W = MOBA_HEADS * MOBA_HEAD_DIM

RET_KERNEL_CHUNK = 256
RET_STEP_HEADS = 4
LANES = 128
MOBA_STEPS = MOBA_W // LANES
MOBA_STEP_HEADS = LANES // MOBA_HEAD_DIM
MOBA_HEAD_FEATS = 16
PROJ_CHUNK = 512
ONES_ROWS = 16
VMEM_LIMIT_BYTES = 56 * 1024 * 1024

F32 = jnp.float32
BF16 = jnp.bfloat16


def _dot(a, b):
    return jnp.dot(a, b, preferred_element_type=F32)


def _rms(x):
    return x * lax.rsqrt(jnp.mean(x * x, axis=-1, keepdims=True) + NORM_EPS)


def _sigmoid(x):
    return 1.0 / (1.0 + jnp.exp(-x))


def _in_proj_layout(d):
    return (("ret_q", RET_HEADS, RET_QK_DIM), ("ret_k", RET_HEADS, RET_QK_DIM),
            ("ret_v", RET_HEADS, RET_V_DIM), ("ret_g", RET_HEADS, RET_V_DIM),
            ("moba_q", MOBA_STEPS, LANES), ("moba_k", MOBA_STEPS, LANES), ("moba_v", MOBA_STEPS, LANES),
            ("gate_ret", 1, d), ("gate_moba", 1, d))


def _in_proj_kernel(x_ref, g_ref, w_ref, dq_ref, dk_ref, *out_refs, layout):
    h = (_rms(x_ref[...]) * g_ref[...]).astype(BF16)
    epilogue = {
        "ret_q": lambda z, c: z * dq_ref[:, c:c + PROJ_CHUNK],
        "ret_k": lambda z, c: z * dk_ref[:, c:c + PROJ_CHUNK],
    }
    off = 0
    for ref, (name, heads, width) in zip(out_refs, layout):
        for c in range(0, heads * width, PROJ_CHUNK):
            zc = _dot(h, w_ref[:, off + c:off + c + PROJ_CHUNK])
            if name in epilogue:
                zc = epilogue[name](zc, c)
            zc = zc.astype(BF16)
            if heads == 1:
                ref[:, c:c + PROJ_CHUNK] = zc
            else:
                for j in range(PROJ_CHUNK // width):
                    ref[0, c // width + j] = zc[:, j * width:(j + 1) * width]
        off += heads * width


def _retention_decay_tables(rows, chunk):
    log_g = np.log1p(-np.exp2(-5.0 - np.arange(RET_HEADS, dtype=np.float64)))
    steps = (np.arange(rows) % chunk + 1.0)[:, None, None] * log_g[None, :, None]
    expand = lambda a: jnp.asarray(np.broadcast_to(a, (rows, RET_HEADS, RET_QK_DIM)).reshape(rows, RET_QK_W)
                                   .astype(np.float32))
    return expand(np.exp(steps)), expand(np.exp(-steps) * RET_QK_DIM ** -0.5)


def _in_proj(x2d, g, w_bf16, *, b, t, tm):
    m, d = x2d.shape
    layout = _in_proj_layout(d)
    assert t % tm == 0 and tm % RET_KERNEL_CHUNK == 0
    assert w_bf16.shape[1] == sum(h * w for _, h, w in layout)
    tiles = t // tm
    dq, dk = _retention_decay_tables(tm, RET_KERNEL_CHUNK)
    out_specs, out_shapes = [], []
    for _, heads, width in layout:
        assert (heads * width) % PROJ_CHUNK == 0 and PROJ_CHUNK % width == 0 or heads == 1
        if heads == 1:
            out_specs.append(pl.BlockSpec((tm, width), lambda i: (i, 0)))
            out_shapes.append(jax.ShapeDtypeStruct((m, width), BF16))
        else:
            out_specs.append(pl.BlockSpec((1, heads, tm, width), lambda i: (i // tiles, 0, i % tiles, 0)))
            out_shapes.append(jax.ShapeDtypeStruct((b, heads, t, width), BF16))
    const = lambda a: pl.BlockSpec(a.shape, lambda i: (0, 0), pipeline_mode=pl.Buffered(1))
    return pl.pallas_call(
        functools.partial(_in_proj_kernel, layout=layout),
        grid=(m // tm,),
        in_specs=[
            pl.BlockSpec((tm, d), lambda i: (i, 0)),
            pl.BlockSpec((1, d), lambda i: (0, 0)),
            const(w_bf16), const(dq), const(dk),
        ],
        out_specs=out_specs,
        out_shape=out_shapes,
        compiler_params=pltpu.CompilerParams(
            dimension_semantics=("parallel",),
            vmem_limit_bytes=VMEM_LIMIT_BYTES),
        name="in_proj",
    )(x2d, g, w_bf16, dq, dk)


def _retention_kernel(cd_ref, q_ref, k_ref, v_ref, g_ref, o_ref, *, chunk):
    heads, t = q_ref.shape[1], q_ref.shape[2]
    dv = v_ref.shape[3]
    row = lax.broadcasted_iota(jnp.int32, (chunk, chunk), 0)
    col = lax.broadcasted_iota(jnp.int32, (chunk, chunk), 1)
    causal = col <= row
    cds = [cd_ref[pl.program_id(1) * heads + h] for h in range(heads)]
    states = [jnp.zeros((q_ref.shape[3], dv), F32) for _ in range(heads)]
    for n in range(t // chunk):
        rows = slice(n * chunk, (n + 1) * chunk)
        for h in range(heads):
            qc = q_ref[0, h, rows, :]
            kc = k_ref[0, h, rows, :]
            vc = v_ref[0, h, rows, :]
            kc_t = kc.T
            scores = jnp.where(causal, _dot(qc, kc_t), 0.0).astype(BF16)
            o = _dot(scores, vc) + _dot(qc, states[h].astype(BF16))
            half_g = 0.5 * g_ref[0, h, rows, :].astype(F32)
            swish = half_g * (1.0 + jnp.tanh(half_g))
            o_ref[0, rows, h * dv:(h + 1) * dv] = (_rms(o) * swish).astype(o_ref.dtype)
            states[h] = cds[h] * (states[h] + _dot(kc_t, vc))


def _retention(q, k, v, g):
    b, heads, t, _ = q.shape
    c = RET_KERNEL_CHUNK
    hs = RET_STEP_HEADS
    log_g = np.log1p(-np.exp2(-5.0 - np.arange(RET_HEADS, dtype=np.float64)))
    cd = jnp.asarray(np.exp(log_g * c).astype(np.float32))
    head_spec = lambda width: pl.BlockSpec((1, hs, t, width), lambda i, h: (i, h, 0, 0))
    return pl.pallas_call(
        functools.partial(_retention_kernel, chunk=c),
        grid=(b, heads // hs),
        in_specs=[
            pl.BlockSpec(memory_space=pltpu.SMEM),
            head_spec(RET_QK_DIM), head_spec(RET_QK_DIM), head_spec(RET_V_DIM), head_spec(RET_V_DIM),
        ],
        out_specs=pl.BlockSpec((1, t, hs * RET_V_DIM), lambda i, h: (i, 0, h)),
        out_shape=jax.ShapeDtypeStruct((b, t, RET_V_W), BF16),
        compiler_params=pltpu.CompilerParams(
            dimension_semantics=("parallel", "arbitrary"),
            vmem_limit_bytes=VMEM_LIMIT_BYTES),
        name="retention",
    )(cd, q, k, v, g)


def _moba_key_features(t, blk):
    s = np.arange(t)
    feats = np.zeros((t, LANES), np.float32)
    nb = t // blk
    for f0 in range(0, MOBA_STEP_HEADS * MOBA_HEAD_FEATS, MOBA_HEAD_FEATS):
        feats[s, f0 + s // blk] = 1.0
        feats[:, f0 + nb] = s % blk
        feats[:, f0 + nb + 1] = s // blk
    return jnp.asarray(feats, dtype=BF16)


def _moba_kernel(slope_ref, q_ref, k_ref, v_ref, kx_ref, o_ref, *, blk, topk, head_dim):
    t = q_ref.shape[2]
    nb = t // blk
    width = q_ref.shape[3]
    heads_per_step = width // head_dim
    scale = head_dim ** -0.5
    q_all = q_ref[0, 0]
    k_all = k_ref[0, 0]
    k_aug = jnp.concatenate([k_all, kx_ref[...]], axis=1)
    v_t = v_ref[0, 0].astype(F32).T.astype(BF16)
    lane = lax.broadcasted_iota(jnp.int32, (1, width), 1)
    in_head = [(lane >= h * head_dim) & (lane < (h + 1) * head_dim) for h in range(heads_per_step)]

    k_mean = jnp.mean(k_all.astype(F32).reshape(nb, blk, width), axis=1)
    km_hi = k_mean.astype(BF16).astype(F32)
    km_mid = (k_mean - km_hi).astype(BF16).astype(F32)
    km_lo = k_mean - km_hi - km_mid
    pieces = [jnp.where(in_head[h], part, 0.0) for h in range(heads_per_step) for part in (km_hi, km_mid, km_lo)]
    q_t = q_all.astype(F32).T
    gates = _dot(jnp.concatenate(pieces, axis=0).astype(BF16), q_t.astype(BF16))

    key_i = lax.broadcasted_iota(jnp.int32, (blk, blk), 0)
    qry_i = lax.broadcasted_iota(jnp.int32, (blk, blk), 1)
    causal_t = key_i <= qry_i

    g_row = lax.broadcasted_iota(jnp.int32, (nb, t), 0)
    g_qblk = lax.broadcasted_iota(jnp.int32, (nb, t), 1) // blk
    eligible = g_row < g_qblk
    own = g_row == g_qblk
    x_row = lax.broadcasted_iota(jnp.int32, (MOBA_HEAD_FEATS - nb, t), 0)

    q_row = lax.broadcasted_iota(jnp.int32, (width, 1), 0)
    q_scaled_t = q_t * scale
    feat_pad = jnp.zeros((MOBA_HEAD_FEATS, t), F32)
    tail_pad = jnp.zeros((LANES - heads_per_step * MOBA_HEAD_FEATS, t), F32)
    ones_rows = jnp.ones((ONES_ROWS, t), BF16)
    q_augs_t, v_exts = [], []
    for head in range(heads_per_step):
        slope = slope_ref[pl.program_id(1) * heads_per_step + head]
        g0 = 3 * head * nb
        gate = gates[g0:g0 + nb] + gates[g0 + nb:g0 + 2 * nb] + gates[g0 + 2 * nb:g0 + 3 * nb]
        gate = jnp.where(eligible, gate, -jnp.inf)
        rank = jnp.zeros((nb, t), jnp.int32)
        for m in range(nb):
            g_m = gate[m:m + 1, :]
            beats = (g_m > gate) | ((g_m == gate) & (m < g_row))
            rank = rank + beats.astype(jnp.int32)
        selected = eligible & (rank < topk) & (jnp.abs(gate) < jnp.inf)
        sel_bias_t = jnp.where(selected | own, 0.0, MASK_VALUE)
        slope_rows = jnp.where(x_row == 0, slope, jnp.where(x_row == 1, slope * float(blk), 0.0))
        in_head_rows = (q_row >= head * head_dim) & (q_row < (head + 1) * head_dim)
        parts = [jnp.where(in_head_rows, q_scaled_t, 0.0)]
        parts += [feat_pad] * head + [sel_bias_t, slope_rows] + [feat_pad] * (heads_per_step - 1 - head)
        parts.append(tail_pad)
        q_augs_t.append(jnp.concatenate(parts, axis=0).astype(BF16))
        v_exts.append(jnp.concatenate([v_t[head * head_dim:(head + 1) * head_dim, :], ones_rows], axis=0))

    def scores(qb, head):
        nk = (qb + 1) * blk
        return _dot(k_aug[:nk], q_augs_t[head][:, qb * blk:nk])

    def attend(qb, head, s_t):
        past = qb * blk
        diag = jnp.where(causal_t, s_t[past:], MASK_VALUE)
        m_col = jnp.max(diag, axis=0, keepdims=True)
        if qb:
            m_col = jnp.maximum(m_col, jnp.max(s_t[:past], axis=0, keepdims=True))
        v_e = v_exts[head]
        acc = _dot(v_e[:, past:past + blk], jnp.exp((diag - m_col).astype(BF16)))
        if qb:
            acc = acc + _dot(v_e[:, :past], jnp.exp((s_t[:past] - m_col).astype(BF16)))
        return acc[:head_dim] / acc[head_dim:head_dim + 1]

    heads = range(heads_per_step)
    order = list(reversed(range(nb)))
    pending = [scores(order[0], h) for h in heads]
    for i, qb in enumerate(order):
        nxt = [scores(order[i + 1], h) for h in heads] if i + 1 < nb else None
        block_out_t = [attend(qb, h, pending[h]) for h in heads]
        o_ref[0, qb * blk:(qb + 1) * blk, :] = jnp.concatenate(block_out_t, axis=0).T.astype(o_ref.dtype)
        pending = nxt


def _moba(q, k, v):
    b, steps, t, width = q.shape
    assert t % MOBA_BLOCK == 0 and t // MOBA_BLOCK + 2 <= MOBA_HEAD_FEATS
    slopes = jnp.asarray(np.exp2(-8.0 / MOBA_HEADS * np.arange(1, MOBA_HEADS + 1)).astype(np.float32))
    spec = pl.BlockSpec((1, 1, t, width), lambda i, h: (i, h, 0, 0))
    return pl.pallas_call(
        functools.partial(_moba_kernel, blk=MOBA_BLOCK, topk=MOBA_TOPK, head_dim=MOBA_HEAD_DIM),
        grid=(b, steps),
        in_specs=[pl.BlockSpec(memory_space=pltpu.SMEM), spec, spec, spec,
                  pl.BlockSpec((t, LANES), lambda i, h: (0, 0))],
        out_specs=pl.BlockSpec((1, t, width), lambda i, h: (i, 0, h)),
        out_shape=jax.ShapeDtypeStruct((b, t, MOBA_W), BF16),
        compiler_params=pltpu.CompilerParams(
            dimension_semantics=("parallel", "arbitrary"),
            vmem_limit_bytes=VMEM_LIMIT_BYTES),
        name="moba",
    )(slopes, q, k, v, _moba_key_features(t, MOBA_BLOCK))


def _post_kernel(x_ref, ret_ref, mo_ref, gr_ref, gm_ref, bgr_ref, bgm_ref, wro_ref, wmo_ref,
                 wout_ref, g2_ref, wup_ref, wdn_ref, gf_ref, o_ref, *, ff_chunk):
    y_ret = _dot(ret_ref[...], wro_ref[...])
    y_moba = _dot(mo_ref[...], wmo_ref[...])
    gate_r = _sigmoid(gr_ref[...].astype(F32) + bgr_ref[...])
    gate_m = _sigmoid(gm_ref[...].astype(F32) + bgm_ref[...])
    merged = (gate_r * y_ret + gate_m * y_moba).astype(BF16)
    x1 = x_ref[...] + _dot(merged, wout_ref[...])
    h2 = (_rms(x1) * g2_ref[...]).astype(BF16)
    acc = x1
    for c in range(0, wup_ref.shape[1], ff_chunk):
        u = _dot(h2, wup_ref[:, c:c + ff_chunk])
        u = jnp.square(jnp.maximum(u, 0.0)).astype(BF16)
        acc = acc + _dot(u, wdn_ref[c:c + ff_chunk, :])
    o_ref[...] = (_rms(acc) * gf_ref[...]).astype(o_ref.dtype)


def _post(x2d, ret2d, mo2d, gr2d, gm2d, bg_r, bg_m, w_ret_o, w_moba_o, w_out, g2, w_up, w_down, gf,
          *, tm):
    m, d = x2d.shape
    dff = w_up.shape[1]
    row = lambda width: pl.BlockSpec((tm, width), lambda i: (i, 0))
    const = lambda shape: pl.BlockSpec(shape, lambda i: (0, 0), pipeline_mode=pl.Buffered(1))
    return pl.pallas_call(
        functools.partial(_post_kernel, ff_chunk=1024),
        grid=(m // tm,),
        in_specs=[
            row(d), row(RET_V_W), row(MOBA_W), row(d), row(d),
            const((1, d)), const((1, d)),
            const((RET_V_W, d)), const((MOBA_W, d)), const((d, d)),
            const((1, d)), const((d, dff)), const((dff, d)), const((1, d)),
        ],
        out_specs=row(d),
        out_shape=jax.ShapeDtypeStruct((m, d), x2d.dtype),
        compiler_params=pltpu.CompilerParams(
            dimension_semantics=("parallel",),
            vmem_limit_bytes=VMEM_LIMIT_BYTES),
        name="merge_mlp",
    )(x2d, ret2d, mo2d, gr2d, gm2d, bg_r, bg_m, w_ret_o, w_moba_o, w_out, g2, w_up, w_down, gf)


def kernel(x, norm1_g, w_in, b_gate, w_ret_o, w_moba_o, w_out, norm2_g, w_up, w_down, normf_g):
    b, t, d = x.shape
    assert norm1_g.shape[0] == 1, "single-layer block"
    m = b * t
    x2d = x.reshape(m, d)
    rq, rk, rv, rg, mq, mk, mv, gr, gm = _in_proj(x2d, norm1_g, w_in[0].astype(BF16), b=b, t=t, tm=512)
    ret = _retention(rq, rk, rv, rg)
    mo = _moba(mq, mk, mv)
    bg = b_gate[0].reshape(2, 1, d)
    out = _post(x2d, ret.reshape(m, RET_V_W), mo.reshape(m, MOBA_W), gr, gm, bg[0], bg[1],
                w_ret_o[0].astype(BF16), w_moba_o[0].astype(BF16), w_out[0].astype(BF16),
                norm2_g, w_up[0].astype(BF16), w_down[0].astype(BF16), normf_g.reshape(1, d),
                tm=512)
    return out.reshape(b, t, d)
```

```python
import functools

import numpy as np
import jax
import jax.numpy as jnp
from jax import lax
from jax.experimental import pallas as pl
from jax.experimental.pallas import tpu as pltpu

RET_HEADS = 4
RET_QK_DIM = 128
RET_V_DIM = 256
MOBA_HEADS = 8
MOBA_HEAD_DIM = 64
MOBA_BLOCK = 256
MOBA_TOPK = 3
NORM_EPS = 1e-6
MASK_VALUE = -1e30

RET_QK_W = RET_HEADS * RET_QK_DIM
RET_V_W = RET_HEADS * RET_V_DIM
MOBA_W = MOBA_HEADS * MOBA_HEAD_DIM

RET_KERNEL_CHUNK = 256
RET_STEP_HEADS = 4
LANES = 128
MOBA_STEPS = MOBA_W // LANES
MOBA_STEP_HEADS = LANES // MOBA_HEAD_DIM
MOBA_HEAD_FEATS = 16
PROJ_CHUNK = 512
ONES_ROWS = 16
MOBA_BOUND_SLACK = 1.0 + 2.0 ** -6
MOBA_DENOM_FLOOR = 1e-30
VMEM_LIMIT_BYTES = 56 * 1024 * 1024

F32 = jnp.float32
BF16 = jnp.bfloat16


def _dot(a, b):
    return jnp.dot(a, b, preferred_element_type=F32)


def _rms(x):
    return x * lax.rsqrt(jnp.mean(x * x, axis=-1, keepdims=True) + NORM_EPS)


def _sigmoid(x):
    return 1.0 / (1.0 + jnp.exp(-x))


def _in_proj_layout(d):
    return (("ret_q", RET_HEADS, RET_QK_DIM), ("ret_k", RET_HEADS, RET_QK_DIM),
            ("ret_v", RET_HEADS, RET_V_DIM), ("ret_g", RET_HEADS, RET_V_DIM),
            ("moba_q", MOBA_STEPS, LANES), ("moba_k", MOBA_STEPS, LANES), ("moba_v", MOBA_STEPS, LANES),
            ("gate_ret", 1, d), ("gate_moba", 1, d))


def _in_proj_kernel(x_ref, g_ref, w_ref, dq_ref, dk_ref, *out_refs, layout):
    h = (_rms(x_ref[...]) * g_ref[...]).astype(BF16)
    epilogue = {
        "ret_q": lambda z, c: z * dq_ref[:, c:c + PROJ_CHUNK],
        "ret_k": lambda z, c: z * dk_ref[:, c:c + PROJ_CHUNK],
    }
    off = 0
    for ref, (name, heads, width) in zip(out_refs, layout):
        for c in range(0, heads * width, PROJ_CHUNK):
            zc = _dot(h, w_ref[:, off + c:off + c + PROJ_CHUNK])
            if name in epilogue:
                zc = epilogue[name](zc, c)
            zc = zc.astype(BF16)
            if heads == 1:
                ref[:, c:c + PROJ_CHUNK] = zc
            else:
                for j in range(PROJ_CHUNK // width):
                    ref[0, c // width + j] = zc[:, j * width:(j + 1) * width]
        off += heads * width


def _retention_decay_tables(rows, chunk):
    log_g = np.log1p(-np.exp2(-5.0 - np.arange(RET_HEADS, dtype=np.float64)))
    steps = (np.arange(rows) % chunk + 1.0)[:, None, None] * log_g[None, :, None]
    expand = lambda a: jnp.asarray(np.broadcast_to(a, (rows, RET_HEADS, RET_QK_DIM)).reshape(rows, RET_QK_W)
                                   .astype(np.float32))
    return expand(np.exp(steps)), expand(np.exp(-steps) * RET_QK_DIM ** -0.5)


def _in_proj(x2d, g, w_bf16, *, b, t, tm):
    m, d = x2d.shape
    layout = _in_proj_layout(d)
    assert t % tm == 0 and tm % RET_KERNEL_CHUNK == 0
    assert w_bf16.shape[1] == sum(h * w for _, h, w in layout)
    tiles = t // tm
    dq, dk = _retention_decay_tables(tm, RET_KERNEL_CHUNK)
    out_specs, out_shapes = [], []
    for _, heads, width in layout:
        assert (heads * width) % PROJ_CHUNK == 0 and PROJ_CHUNK % width == 0 or heads == 1
        if heads == 1:
            out_specs.append(pl.BlockSpec((tm, width), lambda i: (i, 0)))
            out_shapes.append(jax.ShapeDtypeStruct((m, width), BF16))
        else:
            out_specs.append(pl.BlockSpec((1, heads, tm, width), lambda i: (i // tiles, 0, i % tiles, 0)))
            out_shapes.append(jax.ShapeDtypeStruct((b, heads, t, width), BF16))
    const = lambda a: pl.BlockSpec(a.shape, lambda i: (0, 0), pipeline_mode=pl.Buffered(1))
    return pl.pallas_call(
        functools.partial(_in_proj_kernel, layout=layout),
        grid=(m // tm,),
        in_specs=[
            pl.BlockSpec((tm, d), lambda i: (i, 0)),
            pl.BlockSpec((1, d), lambda i: (0, 0)),
            const(w_bf16), const(dq), const(dk),
        ],
        out_specs=out_specs,
        out_shape=out_shapes,
        compiler_params=pltpu.CompilerParams(
            dimension_semantics=("parallel",),
            vmem_limit_bytes=VMEM_LIMIT_BYTES),
        name="in_proj",
    )(x2d, g, w_bf16, dq, dk)


def _retention_kernel(cd_ref, q_ref, k_ref, v_ref, g_ref, o_ref, *, chunk):
    heads, t = q_ref.shape[1], q_ref.shape[2]
    dv = v_ref.shape[3]
    row = lax.broadcasted_iota(jnp.int32, (chunk, chunk), 0)
    col = lax.broadcasted_iota(jnp.int32, (chunk, chunk), 1)
    causal = col <= row
    cds = [cd_ref[pl.program_id(1) * heads + h] for h in range(heads)]
    states = [jnp.zeros((q_ref.shape[3], dv), F32) for _ in range(heads)]
    for n in range(t // chunk):
        rows = slice(n * chunk, (n + 1) * chunk)
        for h in range(heads):
            qc = q_ref[0, h, rows, :]
            kc = k_ref[0, h, rows, :]
            vc = v_ref[0, h, rows, :]
            kc_t = kc.T
            scores = jnp.where(causal, _dot(qc, kc_t), 0.0).astype(BF16)
            o = _dot(scores, vc) + _dot(qc, states[h].astype(BF16))
            half_g = 0.5 * g_ref[0, h, rows, :].astype(F32)
            swish = half_g * (1.0 + jnp.tanh(half_g))
            o_ref[0, rows, h * dv:(h + 1) * dv] = (_rms(o) * swish).astype(o_ref.dtype)
            states[h] = cds[h] * (states[h] + _dot(kc_t, vc))


def _retention(q, k, v, g):
    b, heads, t, _ = q.shape
    c = RET_KERNEL_CHUNK
    hs = RET_STEP_HEADS
    log_g = np.log1p(-np.exp2(-5.0 - np.arange(RET_HEADS, dtype=np.float64)))
    cd = jnp.asarray(np.exp(log_g * c).astype(np.float32))
    head_spec = lambda width: pl.BlockSpec((1, hs, t, width), lambda i, h: (i, h, 0, 0))
    return pl.pallas_call(
        functools.partial(_retention_kernel, chunk=c),
        grid=(b, heads // hs),
        in_specs=[
            pl.BlockSpec(memory_space=pltpu.SMEM),
            head_spec(RET_QK_DIM), head_spec(RET_QK_DIM), head_spec(RET_V_DIM), head_spec(RET_V_DIM),
        ],
        out_specs=pl.BlockSpec((1, t, hs * RET_V_DIM), lambda i, h: (i, 0, h)),
        out_shape=jax.ShapeDtypeStruct((b, t, RET_V_W), BF16),
        compiler_params=pltpu.CompilerParams(
            dimension_semantics=("parallel", "arbitrary"),
            vmem_limit_bytes=VMEM_LIMIT_BYTES),
        name="retention",
    )(cd, q, k, v, g)


def _moba_key_features(t, blk):
    s = np.arange(t)
    feats = np.zeros((t, LANES), np.float32)
    nb = t // blk
    for f0 in range(0, MOBA_STEP_HEADS * MOBA_HEAD_FEATS, MOBA_HEAD_FEATS):
        feats[s, f0 + s // blk] = 1.0
        feats[:, f0 + nb] = s % blk
        feats[:, f0 + nb + 1] = s // blk
        feats[:, f0 + nb + 2] = 1.0
    return jnp.asarray(feats, dtype=BF16)


def _moba_kernel(slope_ref, q_ref, k_ref, v_ref, kx_ref, o_ref, *, blk, topk, head_dim):
    t = q_ref.shape[2]
    nb = t // blk
    width = q_ref.shape[3]
    heads_per_step = width // head_dim
    scale = head_dim ** -0.5
    q_all = q_ref[0, 0]
    k_all = k_ref[0, 0]
    k_aug = jnp.concatenate([k_all, kx_ref[...]], axis=1)
    v_t = v_ref[0, 0].astype(F32).T.astype(BF16)
    lane = lax.broadcasted_iota(jnp.int32, (1, width), 1)
    in_head = [(lane >= h * head_dim) & (lane < (h + 1) * head_dim) for h in range(heads_per_step)]

    k_mean = jnp.mean(k_all.astype(F32).reshape(nb, blk, width), axis=1)
    km_hi = k_mean.astype(BF16).astype(F32)
    km_mid = (k_mean - km_hi).astype(BF16).astype(F32)
    km_lo = k_mean - km_hi - km_mid
    pieces = [jnp.where(in_head[h], part, 0.0) for h in range(heads_per_step) for part in (km_hi, km_mid, km_lo)]
    q_t = q_all.astype(F32).T
    gates = _dot(jnp.concatenate(pieces, axis=0).astype(BF16), q_t.astype(BF16))

    key_i = lax.broadcasted_iota(jnp.int32, (blk, blk), 0)
    qry_i = lax.broadcasted_iota(jnp.int32, (blk, blk), 1)
    causal_t = key_i <= qry_i

    g_row = lax.broadcasted_iota(jnp.int32, (nb, t), 0)
    g_qblk = lax.broadcasted_iota(jnp.int32, (nb, t), 1) // blk
    eligible = g_row < g_qblk
    own = g_row == g_qblk
    x_row = lax.broadcasted_iota(jnp.int32, (MOBA_HEAD_FEATS - nb, t), 0)

    q_row = lax.broadcasted_iota(jnp.int32, (width, 1), 0)
    k_f32 = k_all.astype(F32)
    k_sq_max = jnp.max(k_f32 * k_f32, axis=0, keepdims=True)
    t_pos = lax.broadcasted_iota(jnp.int32, (1, t), 1).astype(F32)
    q_scaled_t = q_t * scale
    feat_pad = jnp.zeros((MOBA_HEAD_FEATS, t), F32)
    tail_pad = jnp.zeros((LANES - heads_per_step * MOBA_HEAD_FEATS, t), F32)
    ones_rows = jnp.ones((ONES_ROWS, t), BF16)
    q_augs_t, v_exts = [], []
    for head in range(heads_per_step):
        slope = slope_ref[pl.program_id(1) * heads_per_step + head]
        g0 = 3 * head * nb
        gate = gates[g0:g0 + nb] + gates[g0 + nb:g0 + 2 * nb] + gates[g0 + 2 * nb:g0 + 3 * nb]
        gate = jnp.where(eligible, gate, -jnp.inf)
        rank = jnp.zeros((nb, t), jnp.int32)
        for m in range(nb):
            g_m = gate[m:m + 1, :]
            beats = (g_m > gate) | ((g_m == gate) & (m < g_row))
            rank = rank + beats.astype(jnp.int32)
        selected = eligible & (rank < topk) & (jnp.abs(gate) < jnp.inf)
        sel_bias_t = jnp.where(selected | own, 0.0, MASK_VALUE)
        in_head_rows = (q_row >= head * head_dim) & (q_row < (head + 1) * head_dim)
        q_head = jnp.where(in_head_rows, q_t, 0.0)
        q_sq = jnp.sum(q_head * q_head, axis=0, keepdims=True)
        k_sq = jnp.sum(jnp.where(in_head[head], k_sq_max, 0.0), axis=1, keepdims=True)
        bound = (scale * MOBA_BOUND_SLACK) * jnp.sqrt(q_sq * k_sq)
        shift = -(slope * t_pos + bound)
        slope_rows = jnp.where(x_row == 0, slope, jnp.where(x_row == 1, slope * float(blk),
                                                           jnp.where(x_row == 2, shift, 0.0)))
        parts = [jnp.where(in_head_rows, q_scaled_t, 0.0)]
        parts += [feat_pad] * head + [sel_bias_t, slope_rows] + [feat_pad] * (heads_per_step - 1 - head)
        parts.append(tail_pad)
        q_augs_t.append(jnp.concatenate(parts, axis=0).astype(BF16))
        v_exts.append(jnp.concatenate([v_t[head * head_dim:(head + 1) * head_dim, :], ones_rows], axis=0))

    def scores(qb, head):
        nk = (qb + 1) * blk
        return _dot(k_aug[:nk], q_augs_t[head][:, qb * blk:nk])

    def attend(qb, head, s_t):
        past = qb * blk
        diag = jnp.where(causal_t, s_t[past:], MASK_VALUE)
        m_col = jnp.max(diag, axis=0, keepdims=True)
        if qb:
            m_col = jnp.maximum(m_col, jnp.max(s_t[:past], axis=0, keepdims=True))
        v_e = v_exts[head]
        acc = _dot(v_e[:, past:past + blk], jnp.exp((diag - m_col).astype(BF16)))
        if qb:
            acc = acc + _dot(v_e[:, :past], jnp.exp((s_t[:past] - m_col).astype(BF16)))
        denom = acc[head_dim:head_dim + 1]
        return acc[:head_dim] / denom, denom

    def attend_bounded(qb, head, s_t):
        past = qb * blk
        v_e = v_exts[head]
        acc = _dot(v_e[:, past:past + blk], jnp.exp(jnp.where(causal_t, s_t[past:], MASK_VALUE)).astype(BF16))
        if qb:
            acc = acc + _dot(v_e[:, :past], jnp.exp(s_t[:past]).astype(BF16))
        denom = acc[head_dim:head_dim + 1]
        return acc[:head_dim] / denom, denom

    def run(attend_fn):
        heads = range(heads_per_step)
        order = list(reversed(range(nb)))
        denom_min = None
        pending = [scores(order[0], h) for h in heads]
        for i, qb in enumerate(order):
            nxt = [scores(order[i + 1], h) for h in heads] if i + 1 < nb else None
            block_out_t = []
            for h in heads:
                out_t, denom = attend_fn(qb, h, pending[h])
                denom_min = denom if denom_min is None else jnp.minimum(denom_min, denom)
                block_out_t.append(out_t)
            o_ref[0, qb * blk:(qb + 1) * blk, :] = jnp.concatenate(block_out_t, axis=0).T.astype(o_ref.dtype)
            pending = nxt
        return denom_min

    denom_min = run(attend_bounded)
    healthy = jnp.min(denom_min) > MOBA_DENOM_FLOOR

    @pl.when(jnp.logical_not(healthy))
    def _():
        run(attend)


def _moba(q, k, v):
    b, steps, t, width = q.shape
    assert t % MOBA_BLOCK == 0 and t // MOBA_BLOCK + 3 <= MOBA_HEAD_FEATS
    slopes = jnp.asarray(np.exp2(-8.0 / MOBA_HEADS * np.arange(1, MOBA_HEADS + 1)).astype(np.float32))
    spec = pl.BlockSpec((1, 1, t, width), lambda i, h: (i, h, 0, 0))
    return pl.pallas_call(
        functools.partial(_moba_kernel, blk=MOBA_BLOCK, topk=MOBA_TOPK, head_dim=MOBA_HEAD_DIM),
        grid=(b, steps),
        in_specs=[pl.BlockSpec(memory_space=pltpu.SMEM), spec, spec, spec,
                  pl.BlockSpec((t, LANES), lambda i, h: (0, 0))],
        out_specs=pl.BlockSpec((1, t, width), lambda i, h: (i, 0, h)),
        out_shape=jax.ShapeDtypeStruct((b, t, MOBA_W), BF16),
        compiler_params=pltpu.CompilerParams(
            dimension_semantics=("parallel", "arbitrary"),
            vmem_limit_bytes=VMEM_LIMIT_BYTES),
        name="moba",
    )(slopes, q, k, v, _moba_key_features(t, MOBA_BLOCK))


def _post_kernel(x_ref, ret_ref, mo_ref, gr_ref, gm_ref, bgr_ref, bgm_ref, wro_ref, wmo_ref,
                 wout_ref, g2_ref, wup_ref, wdn_ref, gf_ref, o_ref, *, ff_chunk):
    y_ret = _dot(ret_ref[...], wro_ref[...])
    y_moba = _dot(mo_ref[...], wmo_ref[...])
    gate_r = _sigmoid(gr_ref[...].astype(F32) + bgr_ref[...])
    gate_m = _sigmoid(gm_ref[...].astype(F32) + bgm_ref[...])
    merged = (gate_r * y_ret + gate_m * y_moba).astype(BF16)
    x1 = x_ref[...] + _dot(merged, wout_ref[...])
    h2 = (_rms(x1) * g2_ref[...]).astype(BF16)
    acc = x1
    for c in range(0, wup_ref.shape[1], ff_chunk):
        u = _dot(h2, wup_ref[:, c:c + ff_chunk])
        u = jnp.square(jnp.maximum(u, 0.0)).astype(BF16)
        acc = acc + _dot(u, wdn_ref[c:c + ff_chunk, :])
    o_ref[...] = (_rms(acc) * gf_ref[...]).astype(o_ref.dtype)


def _post(x2d, ret2d, mo2d, gr2d, gm2d, bg_r, bg_m, w_ret_o, w_moba_o, w_out, g2, w_up, w_down, gf,
          *, tm):
    m, d = x2d.shape
    dff = w_up.shape[1]
    row = lambda width: pl.BlockSpec((tm, width), lambda i: (i, 0))
    const = lambda shape: pl.BlockSpec(shape, lambda i: (0, 0), pipeline_mode=pl.Buffered(1))
    return pl.pallas_call(
        functools.partial(_post_kernel, ff_chunk=1024),
        grid=(m // tm,),
        in_specs=[
            row(d), row(RET_V_W), row(MOBA_W), row(d), row(d),
            const((1, d)), const((1, d)),
            const((RET_V_W, d)), const((MOBA_W, d)), const((d, d)),
            const((1, d)), const((d, dff)), const((dff, d)), const((1, d)),
        ],
        out_specs=row(d),
        out_shape=jax.ShapeDtypeStruct((m, d), x2d.dtype),
        compiler_params=pltpu.CompilerParams(
            dimension_semantics=("parallel",),
            vmem_limit_bytes=VMEM_LIMIT_BYTES),
        name="merge_mlp",
    )(x2d, ret2d, mo2d, gr2d, gm2d, bg_r, bg_m, w_ret_o, w_moba_o, w_out, g2, w_up, w_down, gf)


def kernel(x, norm1_g, w_in, b_gate, w_ret_o, w_moba_o, w_out, norm2_g, w_up, w_down, normf_g):
    b, t, d = x.shape
    assert norm1_g.shape[0] == 1, "single-layer block"
    m = b * t
    x2d = x.reshape(m, d)
    rq, rk, rv, rg, mq, mk, mv, gr, gm = _in_proj(x2d, norm1_g, w_in[0].astype(BF16), b=b, t=t, tm=512)
    ret = _retention(rq, rk, rv, rg)
    mo = _moba(mq, mk, mv)
    bg = b_gate[0].reshape(2, 1, d)
    out = _post(x2d, ret.reshape(m, RET_V_W), mo.reshape(m, MOBA_W), gr, gm, bg[0], bg[1],
                w_ret_o[0].astype(BF16), w_moba_o[0].astype(BF16), w_out[0].astype(BF16),
                norm2_g, w_up[0].astype(BF16), w_down[0].astype(BF16), normf_g.reshape(1, d),
                tm=512)
    return out.reshape(b, t, d)
```

```python
import functools

import numpy as np
import jax
import jax.numpy as jnp
from jax import lax
from jax.experimental import pallas as pl
from jax.experimental.pallas import tpu as pltpu

RET_HEADS = 4
RET_QK_DIM = 128
RET_V_DIM = 256
MOBA_HEADS = 8
MOBA_HEAD_DIM = 64
MOBA_BLOCK = 256
MOBA_TOPK = 3
NORM_EPS = 1e-6
MASK_VALUE = -1e30

RET_QK_W = RET_HEADS * RET_QK_DIM
RET_V_W = RET_HEADS * RET_V_DIM
MOBA_W = MOBA_HEADS * MOBA_HEAD_DIM

RET_KERNEL_CHUNK = 256
RET_STEP_HEADS = 4
LANES = 128
MOBA_STEPS = MOBA_W // LANES
MOBA_STEP_HEADS = LANES // MOBA_HEAD_DIM
MOBA_HEAD_FEATS = 16
PROJ_CHUNK = 512
ROW_SPLITS = 2
ONES_ROWS = 16
MOBA_BOUND_SLACK = 1.0 + 2.0 ** -6
MOBA_DENOM_FLOOR = 1e-30
VMEM_LIMIT_BYTES = 56 * 1024 * 1024

F32 = jnp.float32
BF16 = jnp.bfloat16


def _dot(a, b):
    return jnp.dot(a, b, preferred_element_type=F32)


def _rms(x):
    return x * lax.rsqrt(jnp.mean(x * x, axis=-1, keepdims=True) + NORM_EPS)


def _sigmoid(x):
    return 1.0 / (1.0 + jnp.exp(-x))


def _in_proj_layout(d):
    return (("ret_q", RET_HEADS, RET_QK_DIM), ("ret_k", RET_HEADS, RET_QK_DIM),
            ("ret_v", RET_HEADS, RET_V_DIM), ("ret_g", RET_HEADS, RET_V_DIM),
            ("moba_q", MOBA_STEPS, LANES), ("moba_k", MOBA_STEPS, LANES), ("moba_v", MOBA_STEPS, LANES),
            ("gate_ret", 1, d), ("gate_moba", 1, d))


def _in_proj_kernel(x_ref, g_ref, w_ref, dq_ref, dk_ref, *out_refs, layout):
    rs = x_ref.shape[0] // ROW_SPLITS
    groups = [slice(i * rs, (i + 1) * rs) for i in range(ROW_SPLITS)]
    hs = [(_rms(x_ref[r, :]) * g_ref[...]).astype(BF16) for r in groups]
    epilogue = {
        "ret_q": lambda z, r, c: z * dq_ref[r, c:c + PROJ_CHUNK],
        "ret_k": lambda z, r, c: z * dk_ref[r, c:c + PROJ_CHUNK],
    }
    off = 0
    for ref, (name, heads, width) in zip(out_refs, layout):
        for c in range(0, heads * width, PROJ_CHUNK):
            for h, r in zip(hs, groups):
                zc = _dot(h, w_ref[:, off + c:off + c + PROJ_CHUNK])
                if name in epilogue:
                    zc = epilogue[name](zc, r, c)
                zc = zc.astype(BF16)
                if heads == 1:
                    ref[r, c:c + PROJ_CHUNK] = zc
                else:
                    for j in range(PROJ_CHUNK // width):
                        ref[0, c // width + j, r, :] = zc[:, j * width:(j + 1) * width]
        off += heads * width


def _retention_decay_tables(rows, chunk):
    log_g = np.log1p(-np.exp2(-5.0 - np.arange(RET_HEADS, dtype=np.float64)))
    steps = (np.arange(rows) % chunk + 1.0)[:, None, None] * log_g[None, :, None]
    expand = lambda a: jnp.asarray(np.broadcast_to(a, (rows, RET_HEADS, RET_QK_DIM)).reshape(rows, RET_QK_W)
                                   .astype(np.float32))
    return expand(np.exp(steps)), expand(np.exp(-steps) * RET_QK_DIM ** -0.5)


def _in_proj(x2d, g, w_bf16, *, b, t, tm):
    m, d = x2d.shape
    layout = _in_proj_layout(d)
    assert t % tm == 0 and tm % RET_KERNEL_CHUNK == 0
    assert w_bf16.shape[1] == sum(h * w for _, h, w in layout)
    tiles = t // tm
    dq, dk = _retention_decay_tables(tm, RET_KERNEL_CHUNK)
    out_specs, out_shapes = [], []
    for _, heads, width in layout:
        assert (heads * width) % PROJ_CHUNK == 0 and PROJ_CHUNK % width == 0 or heads == 1
        if heads == 1:
            out_specs.append(pl.BlockSpec((tm, width), lambda i: (i, 0)))
            out_shapes.append(jax.ShapeDtypeStruct((m, width), BF16))
        else:
            out_specs.append(pl.BlockSpec((1, heads, tm, width), lambda i: (i // tiles, 0, i % tiles, 0)))
            out_shapes.append(jax.ShapeDtypeStruct((b, heads, t, width), BF16))
    const = lambda a: pl.BlockSpec(a.shape, lambda i: (0, 0), pipeline_mode=pl.Buffered(1))
    return pl.pallas_call(
        functools.partial(_in_proj_kernel, layout=layout),
        grid=(m // tm,),
        in_specs=[
            pl.BlockSpec((tm, d), lambda i: (i, 0)),
            pl.BlockSpec((1, d), lambda i: (0, 0)),
            const(w_bf16), const(dq), const(dk),
        ],
        out_specs=out_specs,
        out_shape=out_shapes,
        compiler_params=pltpu.CompilerParams(
            dimension_semantics=("parallel",),
            vmem_limit_bytes=VMEM_LIMIT_BYTES),
        name="in_proj",
    )(x2d, g, w_bf16, dq, dk)


def _retention_kernel(cd_ref, q_ref, k_ref, v_ref, g_ref, o_ref, *, chunk):
    heads, t = q_ref.shape[1], q_ref.shape[2]
    dv = v_ref.shape[3]
    row = lax.broadcasted_iota(jnp.int32, (chunk, chunk), 0)
    col = lax.broadcasted_iota(jnp.int32, (chunk, chunk), 1)
    causal = col <= row
    cds = [cd_ref[pl.program_id(1) * heads + h] for h in range(heads)]
    states = [jnp.zeros((q_ref.shape[3], dv), F32) for _ in range(heads)]
    for n in range(t // chunk):
        rows = slice(n * chunk, (n + 1) * chunk)
        for h in range(heads):
            qc = q_ref[0, h, rows, :]
            kc = k_ref[0, h, rows, :]
            vc = v_ref[0, h, rows, :]
            kc_t = kc.T
            scores = jnp.where(causal, _dot(qc, kc_t), 0.0).astype(BF16)
            o = _dot(scores, vc) + _dot(qc, states[h].astype(BF16))
            half_g = 0.5 * g_ref[0, h, rows, :].astype(F32)
            swish = half_g * (1.0 + jnp.tanh(half_g))
            o_ref[0, rows, h * dv:(h + 1) * dv] = (_rms(o) * swish).astype(o_ref.dtype)
            states[h] = cds[h] * (states[h] + _dot(kc_t, vc))


def _retention(q, k, v, g):
    b, heads, t, _ = q.shape
    c = RET_KERNEL_CHUNK
    hs = RET_STEP_HEADS
    log_g = np.log1p(-np.exp2(-5.0 - np.arange(RET_HEADS, dtype=np.float64)))
    cd = jnp.asarray(np.exp(log_g * c).astype(np.float32))
    head_spec = lambda width: pl.BlockSpec((1, hs, t, width), lambda i, h: (i, h, 0, 0))
    return pl.pallas_call(
        functools.partial(_retention_kernel, chunk=c),
        grid=(b, heads // hs),
        in_specs=[
            pl.BlockSpec(memory_space=pltpu.SMEM),
            head_spec(RET_QK_DIM), head_spec(RET_QK_DIM), head_spec(RET_V_DIM), head_spec(RET_V_DIM),
        ],
        out_specs=pl.BlockSpec((1, t, hs * RET_V_DIM), lambda i, h: (i, 0, h)),
        out_shape=jax.ShapeDtypeStruct((b, t, RET_V_W), BF16),
        compiler_params=pltpu.CompilerParams(
            dimension_semantics=("parallel", "arbitrary"),
            vmem_limit_bytes=VMEM_LIMIT_BYTES),
        name="retention",
    )(cd, q, k, v, g)


def _moba_key_features(t, blk):
    s = np.arange(t)
    feats = np.zeros((t, LANES), np.float32)
    nb = t // blk
    for f0 in range(0, MOBA_STEP_HEADS * MOBA_HEAD_FEATS, MOBA_HEAD_FEATS):
        feats[s, f0 + s // blk] = 1.0
        feats[:, f0 + nb] = s % blk
        feats[:, f0 + nb + 1] = s // blk
        feats[:, f0 + nb + 2] = 1.0
    return jnp.asarray(feats, dtype=BF16)


def _moba_kernel(slope_ref, q_ref, k_ref, v_ref, kx_ref, o_ref, *, blk, topk, head_dim):
    t = q_ref.shape[2]
    nb = t // blk
    width = q_ref.shape[3]
    heads_per_step = width // head_dim
    scale = head_dim ** -0.5
    q_all = q_ref[0, 0]
    k_all = k_ref[0, 0]
    k_aug = jnp.concatenate([k_all, kx_ref[...]], axis=1)
    v_t = v_ref[0, 0].astype(F32).T.astype(BF16)
    lane = lax.broadcasted_iota(jnp.int32, (1, width), 1)
    in_head = [(lane >= h * head_dim) & (lane < (h + 1) * head_dim) for h in range(heads_per_step)]

    k_mean = jnp.mean(k_all.astype(F32).reshape(nb, blk, width), axis=1)
    km_hi = k_mean.astype(BF16).astype(F32)
    km_mid = (k_mean - km_hi).astype(BF16).astype(F32)
    km_lo = k_mean - km_hi - km_mid
    pieces = [jnp.where(in_head[h], part, 0.0) for h in range(heads_per_step) for part in (km_hi, km_mid, km_lo)]
    q_t = q_all.astype(F32).T
    gates = _dot(jnp.concatenate(pieces, axis=0).astype(BF16), q_t.astype(BF16))

    key_i = lax.broadcasted_iota(jnp.int32, (blk, blk), 0)
    qry_i = lax.broadcasted_iota(jnp.int32, (blk, blk), 1)
    causal_t = key_i <= qry_i

    g_row = lax.broadcasted_iota(jnp.int32, (nb, t), 0)
    g_qblk = lax.broadcasted_iota(jnp.int32, (nb, t), 1) // blk
    eligible = g_row < g_qblk
    own = g_row == g_qblk
    x_row = lax.broadcasted_iota(jnp.int32, (MOBA_HEAD_FEATS - nb, t), 0)

    q_row = lax.broadcasted_iota(jnp.int32, (width, 1), 0)
    k_f32 = k_all.astype(F32)
    k_sq_max = jnp.max(k_f32 * k_f32, axis=0, keepdims=True)
    t_pos = lax.broadcasted_iota(jnp.int32, (1, t), 1).astype(F32)
    q_scaled_t = q_t * scale
    feat_pad = jnp.zeros((MOBA_HEAD_FEATS, t), F32)
    tail_pad = jnp.zeros((LANES - heads_per_step * MOBA_HEAD_FEATS, t), F32)
    ones_rows = jnp.ones((ONES_ROWS, t), BF16)
    q_augs_t, v_exts = [], []
    for head in range(heads_per_step):
        slope = slope_ref[pl.program_id(1) * heads_per_step + head]
        g0 = 3 * head * nb
        gate = gates[g0:g0 + nb] + gates[g0 + nb:g0 + 2 * nb] + gates[g0 + 2 * nb:g0 + 3 * nb]
        gate = jnp.where(eligible, gate, -jnp.inf)
        rank = jnp.zeros((nb, t), jnp.int32)
        for m in range(nb):
            g_m = gate[m:m + 1, :]
            beats = (g_m > gate) | ((g_m == gate) & (m < g_row))
            rank = rank + beats.astype(jnp.int32)
        selected = eligible & (rank < topk) & (jnp.abs(gate) < jnp.inf)
        sel_bias_t = jnp.where(selected | own, 0.0, MASK_VALUE)
        in_head_rows = (q_row >= head * head_dim) & (q_row < (head + 1) * head_dim)
        q_head = jnp.where(in_head_rows, q_t, 0.0)
        q_sq = jnp.sum(q_head * q_head, axis=0, keepdims=True)
        k_sq = jnp.sum(jnp.where(in_head[head], k_sq_max, 0.0), axis=1, keepdims=True)
        bound = (scale * MOBA_BOUND_SLACK) * jnp.sqrt(q_sq * k_sq)
        shift = -(slope * t_pos + bound)
        slope_rows = jnp.where(x_row == 0, slope, jnp.where(x_row == 1, slope * float(blk),
                                                           jnp.where(x_row == 2, shift, 0.0)))
        parts = [jnp.where(in_head_rows, q_scaled_t, 0.0)]
        parts += [feat_pad] * head + [sel_bias_t, slope_rows] + [feat_pad] * (heads_per_step - 1 - head)
        parts.append(tail_pad)
        q_augs_t.append(jnp.concatenate(parts, axis=0).astype(BF16))
        v_exts.append(jnp.concatenate([v_t[head * head_dim:(head + 1) * head_dim, :], ones_rows], axis=0))

    def scores(qb, head):
        nk = (qb + 1) * blk
        return _dot(k_aug[:nk], q_augs_t[head][:, qb * blk:nk])

    def attend(qb, head, s_t):
        past = qb * blk
        diag = jnp.where(causal_t, s_t[past:], MASK_VALUE)
        m_col = jnp.max(diag, axis=0, keepdims=True)
        if qb:
            m_col = jnp.maximum(m_col, jnp.max(s_t[:past], axis=0, keepdims=True))
        v_e = v_exts[head]
        acc = _dot(v_e[:, past:past + blk], jnp.exp((diag - m_col).astype(BF16)))
        if qb:
            acc = acc + _dot(v_e[:, :past], jnp.exp((s_t[:past] - m_col).astype(BF16)))
        denom = acc[head_dim:head_dim + 1]
        return acc[:head_dim] / denom, denom

    def attend_bounded(qb, head, s_t):
        past = qb * blk
        v_e = v_exts[head]
        acc = _dot(v_e[:, past:past + blk], jnp.exp(jnp.where(causal_t, s_t[past:], MASK_VALUE)).astype(BF16))
        if qb:
            acc = acc + _dot(v_e[:, :past], jnp.exp(s_t[:past]).astype(BF16))
        denom = acc[head_dim:head_dim + 1]
        return acc[:head_dim] / denom, denom

    def run(attend_fn):
        heads = range(heads_per_step)
        order = list(reversed(range(nb)))
        denom_min = None
        pending = [scores(order[0], h) for h in heads]
        for i, qb in enumerate(order):
            nxt = [scores(order[i + 1], h) for h in heads] if i + 1 < nb else None
            block_out_t = []
            for h in heads:
                out_t, denom = attend_fn(qb, h, pending[h])
                denom_min = denom if denom_min is None else jnp.minimum(denom_min, denom)
                block_out_t.append(out_t)
            o_ref[0, qb * blk:(qb + 1) * blk, :] = jnp.concatenate(block_out_t, axis=0).T.astype(o_ref.dtype)
            pending = nxt
        return denom_min

    denom_min = run(attend_bounded)
    healthy = jnp.min(denom_min) > MOBA_DENOM_FLOOR

    @pl.when(jnp.logical_not(healthy))
    def _():
        run(attend)


def _moba(q, k, v):
    b, steps, t, width = q.shape
    assert t % MOBA_BLOCK == 0 and t // MOBA_BLOCK + 3 <= MOBA_HEAD_FEATS
    slopes = jnp.asarray(np.exp2(-8.0 / MOBA_HEADS * np.arange(1, MOBA_HEADS + 1)).astype(np.float32))
    spec = pl.BlockSpec((1, 1, t, width), lambda i, h: (i, h, 0, 0))
    return pl.pallas_call(
        functools.partial(_moba_kernel, blk=MOBA_BLOCK, topk=MOBA_TOPK, head_dim=MOBA_HEAD_DIM),
        grid=(b, steps),
        in_specs=[pl.BlockSpec(memory_space=pltpu.SMEM), spec, spec, spec,
                  pl.BlockSpec((t, LANES), lambda i, h: (0, 0))],
        out_specs=pl.BlockSpec((1, t, width), lambda i, h: (i, 0, h)),
        out_shape=jax.ShapeDtypeStruct((b, t, MOBA_W), BF16),
        compiler_params=pltpu.CompilerParams(
            dimension_semantics=("parallel", "arbitrary"),
            vmem_limit_bytes=VMEM_LIMIT_BYTES),
        name="moba",
    )(slopes, q, k, v, _moba_key_features(t, MOBA_BLOCK))


def _post_kernel(x_ref, ret_ref, mo_ref, gr_ref, gm_ref, bgr_ref, bgm_ref, wro_ref, wmo_ref,
                 wout_ref, g2_ref, wup_ref, wdn_ref, gf_ref, o_ref, *, ff_chunk):
    rs = x_ref.shape[0] // ROW_SPLITS
    groups = [slice(i * rs, (i + 1) * rs) for i in range(ROW_SPLITS)]
    y_ret = [_dot(ret_ref[r, :], wro_ref[...]) for r in groups]
    y_moba = [_dot(mo_ref[r, :], wmo_ref[...]) for r in groups]
    merged = []
    for i, r in enumerate(groups):
        gate_r = _sigmoid(gr_ref[r, :].astype(F32) + bgr_ref[...])
        gate_m = _sigmoid(gm_ref[r, :].astype(F32) + bgm_ref[...])
        merged.append((gate_r * y_ret[i] + gate_m * y_moba[i]).astype(BF16))
    x1 = [x_ref[r, :] + _dot(merged[i], wout_ref[...]) for i, r in enumerate(groups)]
    h2 = [(_rms(x) * g2_ref[...]).astype(BF16) for x in x1]
    acc = list(x1)
    for c in range(0, wup_ref.shape[1], ff_chunk):
        u = [_dot(h, wup_ref[:, c:c + ff_chunk]) for h in h2]
        u = [jnp.square(jnp.maximum(v, 0.0)).astype(BF16) for v in u]
        acc = [a + _dot(v, wdn_ref[c:c + ff_chunk, :]) for a, v in zip(acc, u)]
    for a, r in zip(acc, groups):
        o_ref[r, :] = (_rms(a) * gf_ref[...]).astype(o_ref.dtype)


def _post(x2d, ret2d, mo2d, gr2d, gm2d, bg_r, bg_m, w_ret_o, w_moba_o, w_out, g2, w_up, w_down, gf,
          *, tm):
    m, d = x2d.shape
    dff = w_up.shape[1]
    row = lambda width: pl.BlockSpec((tm, width), lambda i: (i, 0))
    const = lambda shape: pl.BlockSpec(shape, lambda i: (0, 0), pipeline_mode=pl.Buffered(1))
    return pl.pallas_call(
        functools.partial(_post_kernel, ff_chunk=1024),
        grid=(m // tm,),
        in_specs=[
            row(d), row(RET_V_W), row(MOBA_W), row(d), row(d),
            const((1, d)), const((1, d)),
            const((RET_V_W, d)), const((MOBA_W, d)), const((d, d)),
            const((1, d)), const((d, dff)), const((dff, d)), const((1, d)),
        ],
        out_specs=row(d),
        out_shape=jax.ShapeDtypeStruct((m, d), x2d.dtype),
        compiler_params=pltpu.CompilerParams(
            dimension_semantics=("parallel",),
            vmem_limit_bytes=VMEM_LIMIT_BYTES),
        name="merge_mlp",
    )(x2d, ret2d, mo2d, gr2d, gm2d, bg_r, bg_m, w_ret_o, w_moba_o, w_out, g2, w_up, w_down, gf)


def kernel(x, norm1_g, w_in, b_gate, w_ret_o, w_moba_o, w_out, norm2_g, w_up, w_down, normf_g):
    b, t, d = x.shape
    assert norm1_g.shape[0] == 1, "single-layer block"
    m = b * t
    x2d = x.reshape(m, d)
    rq, rk, rv, rg, mq, mk, mv, gr, gm = _in_proj(x2d, norm1_g, w_in[0].astype(BF16), b=b, t=t, tm=512)
    ret = _retention(rq, rk, rv, rg)
    mo = _moba(mq, mk, mv)
    bg = b_gate[0].reshape(2, 1, d)
    out = _post(x2d, ret.reshape(m, RET_V_W), mo.reshape(m, MOBA_W), gr, gm, bg[0], bg[1],
                w_ret_o[0].astype(BF16), w_moba_o[0].astype(BF16), w_out[0].astype(BF16),
                norm2_g, w_up[0].astype(BF16), w_down[0].astype(BF16), normf_g.reshape(1, d),
                tm=512)
    return out.reshape(b, t, d)
```

```python
import functools

import numpy as np
import jax
import jax.numpy as jnp
from jax import lax
from jax.experimental import pallas as pl
from jax.experimental.pallas import tpu as pltpu

RET_HEADS = 4
RET_QK_DIM = 128
RET_V_DIM = 256
MOBA_HEADS = 8
MOBA_HEAD_DIM = 64
MOBA_BLOCK = 256
MOBA_TOPK = 3
NORM_EPS = 1e-6
MASK_VALUE = -1e30

RET_QK_W = RET_HEADS * RET_QK_DIM
RET_V_W = RET_HEADS * RET_V_DIM
MOBA_W = MOBA_HEADS * MOBA_HEAD_DIM

RET_KERNEL_CHUNK = 256
RET_STEP_HEADS = 4
LANES = 128
MOBA_STEPS = MOBA_W // LANES
MOBA_STEP_HEADS = LANES // MOBA_HEAD_DIM
MOBA_STEP_PAIRS = 4
MOBA_HEAD_FEATS = 16
PROJ_CHUNK = 512
ROW_SPLITS = 2
ONES_ROWS = 16
MOBA_BOUND_SLACK = 1.0 + 2.0 ** -6
MOBA_DENOM_FLOOR = 1e-30
VMEM_LIMIT_BYTES = 56 * 1024 * 1024

F32 = jnp.float32
BF16 = jnp.bfloat16


def _dot(a, b):
    return jnp.dot(a, b, preferred_element_type=F32)


def _rms(x):
    return x * lax.rsqrt(jnp.mean(x * x, axis=-1, keepdims=True) + NORM_EPS)


def _sigmoid(x):
    return 1.0 / (1.0 + jnp.exp(-x))


def _in_proj_layout(d):
    return (("ret_q", RET_HEADS, RET_QK_DIM), ("ret_k", RET_HEADS, RET_QK_DIM),
            ("ret_v", RET_HEADS, RET_V_DIM), ("ret_g", RET_HEADS, RET_V_DIM),
            ("moba_q", MOBA_STEPS, LANES), ("moba_k", MOBA_STEPS, LANES), ("moba_v", MOBA_STEPS, LANES),
            ("gate_ret", 1, d), ("gate_moba", 1, d))


def _in_proj_kernel(x_ref, g_ref, w_ref, dq_ref, dk_ref, *out_refs, layout):
    rs = x_ref.shape[0] // ROW_SPLITS
    groups = [slice(i * rs, (i + 1) * rs) for i in range(ROW_SPLITS)]
    hs = [(_rms(x_ref[r, :]) * g_ref[...]).astype(BF16) for r in groups]
    epilogue = {
        "ret_q": lambda z, r, c: z * dq_ref[r, c:c + PROJ_CHUNK],
        "ret_k": lambda z, r, c: z * dk_ref[r, c:c + PROJ_CHUNK],
    }
    off = 0
    for ref, (name, heads, width) in zip(out_refs, layout):
        for c in range(0, heads * width, PROJ_CHUNK):
            for h, r in zip(hs, groups):
                zc = _dot(h, w_ref[:, off + c:off + c + PROJ_CHUNK])
                if name in epilogue:
                    zc = epilogue[name](zc, r, c)
                zc = zc.astype(BF16)
                if heads == 1:
                    ref[r, c:c + PROJ_CHUNK] = zc
                else:
                    for j in range(PROJ_CHUNK // width):
                        ref[0, c // width + j, r, :] = zc[:, j * width:(j + 1) * width]
        off += heads * width


def _retention_decay_tables(rows, chunk):
    log_g = np.log1p(-np.exp2(-5.0 - np.arange(RET_HEADS, dtype=np.float64)))
    steps = (np.arange(rows) % chunk + 1.0)[:, None, None] * log_g[None, :, None]
    expand = lambda a: jnp.asarray(np.broadcast_to(a, (rows, RET_HEADS, RET_QK_DIM)).reshape(rows, RET_QK_W)
                                   .astype(np.float32))
    return expand(np.exp(steps)), expand(np.exp(-steps) * RET_QK_DIM ** -0.5)


def _in_proj(x2d, g, w_bf16, *, b, t, tm):
    m, d = x2d.shape
    layout = _in_proj_layout(d)
    assert t % tm == 0 and tm % RET_KERNEL_CHUNK == 0
    assert w_bf16.shape[1] == sum(h * w for _, h, w in layout)
    tiles = t // tm
    dq, dk = _retention_decay_tables(tm, RET_KERNEL_CHUNK)
    out_specs, out_shapes = [], []
    for _, heads, width in layout:
        assert (heads * width) % PROJ_CHUNK == 0 and PROJ_CHUNK % width == 0 or heads == 1
        if heads == 1:
            out_specs.append(pl.BlockSpec((tm, width), lambda i: (i, 0)))
            out_shapes.append(jax.ShapeDtypeStruct((m, width), BF16))
        else:
            out_specs.append(pl.BlockSpec((1, heads, tm, width), lambda i: (i // tiles, 0, i % tiles, 0)))
            out_shapes.append(jax.ShapeDtypeStruct((b, heads, t, width), BF16))
    const = lambda a: pl.BlockSpec(a.shape, lambda i: (0, 0), pipeline_mode=pl.Buffered(1))
    return pl.pallas_call(
        functools.partial(_in_proj_kernel, layout=layout),
        grid=(m // tm,),
        in_specs=[
            pl.BlockSpec((tm, d), lambda i: (i, 0)),
            pl.BlockSpec((1, d), lambda i: (0, 0)),
            const(w_bf16), const(dq), const(dk),
        ],
        out_specs=out_specs,
        out_shape=out_shapes,
        compiler_params=pltpu.CompilerParams(
            dimension_semantics=("parallel",),
            vmem_limit_bytes=VMEM_LIMIT_BYTES),
        name="in_proj",
    )(x2d, g, w_bf16, dq, dk)


def _retention_kernel(cd_ref, q_ref, k_ref, v_ref, g_ref, o_ref, *, chunk):
    heads, t = q_ref.shape[1], q_ref.shape[2]
    dv = v_ref.shape[3]
    row = lax.broadcasted_iota(jnp.int32, (chunk, chunk), 0)
    col = lax.broadcasted_iota(jnp.int32, (chunk, chunk), 1)
    causal = col <= row
    cds = [cd_ref[pl.program_id(1) * heads + h] for h in range(heads)]
    states = [jnp.zeros((q_ref.shape[3], dv), F32) for _ in range(heads)]
    for n in range(t // chunk):
        rows = slice(n * chunk, (n + 1) * chunk)
        for h in range(heads):
            qc = q_ref[0, h, rows, :]
            kc = k_ref[0, h, rows, :]
            vc = v_ref[0, h, rows, :]
            kc_t = kc.T
            scores = jnp.where(causal, _dot(qc, kc_t), 0.0).astype(BF16)
            o = _dot(scores, vc) + _dot(qc, states[h].astype(BF16))
            half_g = 0.5 * g_ref[0, h, rows, :].astype(F32)
            swish = half_g * (1.0 + jnp.tanh(half_g))
            o_ref[0, rows, h * dv:(h + 1) * dv] = (_rms(o) * swish).astype(o_ref.dtype)
            states[h] = cds[h] * (states[h] + _dot(kc_t, vc))


def _retention(q, k, v, g):
    b, heads, t, _ = q.shape
    c = RET_KERNEL_CHUNK
    hs = RET_STEP_HEADS
    log_g = np.log1p(-np.exp2(-5.0 - np.arange(RET_HEADS, dtype=np.float64)))
    cd = jnp.asarray(np.exp(log_g * c).astype(np.float32))
    head_spec = lambda width: pl.BlockSpec((1, hs, t, width), lambda i, h: (i, h, 0, 0))
    return pl.pallas_call(
        functools.partial(_retention_kernel, chunk=c),
        grid=(b, heads // hs),
        in_specs=[
            pl.BlockSpec(memory_space=pltpu.SMEM),
            head_spec(RET_QK_DIM), head_spec(RET_QK_DIM), head_spec(RET_V_DIM), head_spec(RET_V_DIM),
        ],
        out_specs=pl.BlockSpec((1, t, hs * RET_V_DIM), lambda i, h: (i, 0, h)),
        out_shape=jax.ShapeDtypeStruct((b, t, RET_V_W), BF16),
        compiler_params=pltpu.CompilerParams(
            dimension_semantics=("parallel", "arbitrary"),
            vmem_limit_bytes=VMEM_LIMIT_BYTES),
        name="retention",
    )(cd, q, k, v, g)


def _moba_key_features(t, blk):
    s = np.arange(t)
    feats = np.zeros((t, LANES), np.float32)
    nb = t // blk
    for f0 in range(0, MOBA_STEP_HEADS * MOBA_HEAD_FEATS, MOBA_HEAD_FEATS):
        feats[s, f0 + s // blk] = 1.0
        feats[:, f0 + nb] = s % blk
        feats[:, f0 + nb + 1] = s // blk
        feats[:, f0 + nb + 2] = 1.0
    return jnp.asarray(feats, dtype=BF16)


def _moba_kernel(slope_ref, q_ref, k_ref, v_ref, kx_ref, o_ref, *, blk, topk, head_dim):
    for pair in range(q_ref.shape[1]):
        _moba_pair(slope_ref, q_ref, k_ref, v_ref, kx_ref, o_ref, pair, blk=blk, topk=topk, head_dim=head_dim)


def _moba_pair(slope_ref, q_ref, k_ref, v_ref, kx_ref, o_ref, pair, *, blk, topk, head_dim):
    t = q_ref.shape[2]
    nb = t // blk
    width = q_ref.shape[3]
    heads_per_step = width // head_dim
    scale = head_dim ** -0.5
    q_all = q_ref[0, pair]
    k_all = k_ref[0, pair]
    k_aug = jnp.concatenate([k_all, kx_ref[...]], axis=1)
    v_t = v_ref[0, pair].astype(F32).T.astype(BF16)
    lane = lax.broadcasted_iota(jnp.int32, (1, width), 1)
    in_head = [(lane >= h * head_dim) & (lane < (h + 1) * head_dim) for h in range(heads_per_step)]

    k_mean = jnp.mean(k_all.astype(F32).reshape(nb, blk, width), axis=1)
    km_hi = k_mean.astype(BF16).astype(F32)
    km_mid = (k_mean - km_hi).astype(BF16).astype(F32)
    km_lo = k_mean - km_hi - km_mid
    pieces = [jnp.where(in_head[h], part, 0.0) for h in range(heads_per_step) for part in (km_hi, km_mid, km_lo)]
    q_t = q_all.astype(F32).T
    gates = _dot(jnp.concatenate(pieces, axis=0).astype(BF16), q_t.astype(BF16))

    key_i = lax.broadcasted_iota(jnp.int32, (blk, blk), 0)
    qry_i = lax.broadcasted_iota(jnp.int32, (blk, blk), 1)
    causal_t = key_i <= qry_i

    g_row = lax.broadcasted_iota(jnp.int32, (nb, t), 0)
    g_qblk = lax.broadcasted_iota(jnp.int32, (nb, t), 1) // blk
    eligible = g_row < g_qblk
    own = g_row == g_qblk
    x_row = lax.broadcasted_iota(jnp.int32, (MOBA_HEAD_FEATS - nb, t), 0)

    q_row = lax.broadcasted_iota(jnp.int32, (width, 1), 0)
    k_f32 = k_all.astype(F32)
    k_sq_max = jnp.max(k_f32 * k_f32, axis=0, keepdims=True)
    t_pos = lax.broadcasted_iota(jnp.int32, (1, t), 1).astype(F32)
    q_scaled_t = q_t * scale
    feat_pad = jnp.zeros((MOBA_HEAD_FEATS, t), F32)
    tail_pad = jnp.zeros((LANES - heads_per_step * MOBA_HEAD_FEATS, t), F32)
    ones_rows = jnp.ones((ONES_ROWS, t), BF16)
    q_augs_t, v_exts = [], []
    for head in range(heads_per_step):
        slope = slope_ref[(pl.program_id(1) * q_ref.shape[1] + pair) * heads_per_step + head]
        g0 = 3 * head * nb
        gate = gates[g0:g0 + nb] + gates[g0 + nb:g0 + 2 * nb] + gates[g0 + 2 * nb:g0 + 3 * nb]
        gate = jnp.where(eligible, gate, -jnp.inf)
        rank = jnp.zeros((nb, t), jnp.int32)
        for m in range(nb):
            g_m = gate[m:m + 1, :]
            beats = (g_m > gate) | ((g_m == gate) & (m < g_row))
            rank = rank + beats.astype(jnp.int32)
        selected = eligible & (rank < topk) & (jnp.abs(gate) < jnp.inf)
        sel_bias_t = jnp.where(selected | own, 0.0, MASK_VALUE)
        in_head_rows = (q_row >= head * head_dim) & (q_row < (head + 1) * head_dim)
        q_head = jnp.where(in_head_rows, q_t, 0.0)
        q_sq = jnp.sum(q_head * q_head, axis=0, keepdims=True)
        k_sq = jnp.sum(jnp.where(in_head[head], k_sq_max, 0.0), axis=1, keepdims=True)
        bound = (scale * MOBA_BOUND_SLACK) * jnp.sqrt(q_sq * k_sq)
        shift = -(slope * t_pos + bound)
        slope_rows = jnp.where(x_row == 0, slope, jnp.where(x_row == 1, slope * float(blk),
                                                           jnp.where(x_row == 2, shift, 0.0)))
        parts = [jnp.where(in_head_rows, q_scaled_t, 0.0)]
        parts += [feat_pad] * head + [sel_bias_t, slope_rows] + [feat_pad] * (heads_per_step - 1 - head)
        parts.append(tail_pad)
        q_augs_t.append(jnp.concatenate(parts, axis=0).astype(BF16))
        v_exts.append(jnp.concatenate([v_t[head * head_dim:(head + 1) * head_dim, :], ones_rows], axis=0))

    def scores(qb, head):
        nk = (qb + 1) * blk
        return _dot(k_aug[:nk], q_augs_t[head][:, qb * blk:nk])

    def attend(qb, head, s_t):
        past = qb * blk
        diag = jnp.where(causal_t, s_t[past:], MASK_VALUE)
        m_col = jnp.max(diag, axis=0, keepdims=True)
        if qb:
            m_col = jnp.maximum(m_col, jnp.max(s_t[:past], axis=0, keepdims=True))
        v_e = v_exts[head]
        acc = _dot(v_e[:, past:past + blk], jnp.exp((diag - m_col).astype(BF16)))
        if qb:
            acc = acc + _dot(v_e[:, :past], jnp.exp((s_t[:past] - m_col).astype(BF16)))
        denom = acc[head_dim:head_dim + 1]
        return acc[:head_dim] / denom, denom

    def attend_bounded(qb, head, s_t):
        past = qb * blk
        v_e = v_exts[head]
        acc = _dot(v_e[:, past:past + blk], jnp.exp(jnp.where(causal_t, s_t[past:], MASK_VALUE)).astype(BF16))
        if qb:
            acc = acc + _dot(v_e[:, :past], jnp.exp(s_t[:past]).astype(BF16))
        denom = acc[head_dim:head_dim + 1]
        return acc[:head_dim] / denom, denom

    def run(attend_fn):
        heads = range(heads_per_step)
        order = list(reversed(range(nb)))
        denom_min = None
        pending = [scores(order[0], h) for h in heads]
        for i, qb in enumerate(order):
            nxt = [scores(order[i + 1], h) for h in heads] if i + 1 < nb else None
            block_out_t = []
            for h in heads:
                out_t, denom = attend_fn(qb, h, pending[h])
                denom_min = denom if denom_min is None else jnp.minimum(denom_min, denom)
                block_out_t.append(out_t)
            o_ref[0, qb * blk:(qb + 1) * blk, pair * width:(pair + 1) * width] = (
                jnp.concatenate(block_out_t, axis=0).T.astype(o_ref.dtype))
            pending = nxt
        return denom_min

    denom_min = run(attend_bounded)
    healthy = jnp.min(denom_min) > MOBA_DENOM_FLOOR

    @pl.when(jnp.logical_not(healthy))
    def _():
        run(attend)


def _moba(q, k, v):
    b, steps, t, width = q.shape
    assert t % MOBA_BLOCK == 0 and t // MOBA_BLOCK + 3 <= MOBA_HEAD_FEATS
    slopes = jnp.asarray(np.exp2(-8.0 / MOBA_HEADS * np.arange(1, MOBA_HEADS + 1)).astype(np.float32))
    pp = MOBA_STEP_PAIRS
    assert steps % pp == 0
    spec = pl.BlockSpec((1, pp, t, width), lambda i, h: (i, h, 0, 0))
    return pl.pallas_call(
        functools.partial(_moba_kernel, blk=MOBA_BLOCK, topk=MOBA_TOPK, head_dim=MOBA_HEAD_DIM),
        grid=(b, steps // pp),
        in_specs=[pl.BlockSpec(memory_space=pltpu.SMEM), spec, spec, spec,
                  pl.BlockSpec((t, LANES), lambda i, h: (0, 0))],
        out_specs=pl.BlockSpec((1, t, pp * width), lambda i, h: (i, 0, h)),
        out_shape=jax.ShapeDtypeStruct((b, t, MOBA_W), BF16),
        compiler_params=pltpu.CompilerParams(
            dimension_semantics=("parallel", "arbitrary"),
            vmem_limit_bytes=VMEM_LIMIT_BYTES),
        name="moba",
    )(slopes, q, k, v, _moba_key_features(t, MOBA_BLOCK))


def _post_kernel(x_ref, ret_ref, mo_ref, gr_ref, gm_ref, bgr_ref, bgm_ref, wro_ref, wmo_ref,
                 wout_ref, g2_ref, wup_ref, wdn_ref, gf_ref, o_ref, *, ff_chunk):
    rs = x_ref.shape[0] // ROW_SPLITS
    groups = [slice(i * rs, (i + 1) * rs) for i in range(ROW_SPLITS)]
    y_ret = [_dot(ret_ref[r, :], wro_ref[...]) for r in groups]
    y_moba = [_dot(mo_ref[r, :], wmo_ref[...]) for r in groups]
    merged = []
    for i, r in enumerate(groups):
        gate_r = _sigmoid(gr_ref[r, :].astype(F32) + bgr_ref[...])
        gate_m = _sigmoid(gm_ref[r, :].astype(F32) + bgm_ref[...])
        merged.append((gate_r * y_ret[i] + gate_m * y_moba[i]).astype(BF16))
    x1 = [x_ref[r, :] + _dot(merged[i], wout_ref[...]) for i, r in enumerate(groups)]
    h2 = [(_rms(x) * g2_ref[...]).astype(BF16) for x in x1]
    acc = list(x1)
    for c in range(0, wup_ref.shape[1], ff_chunk):
        u = [_dot(h, wup_ref[:, c:c + ff_chunk]) for h in h2]
        u = [jnp.square(jnp.maximum(v, 0.0)).astype(BF16) for v in u]
        acc = [a + _dot(v, wdn_ref[c:c + ff_chunk, :]) for a, v in zip(acc, u)]
    for a, r in zip(acc, groups):
        o_ref[r, :] = (_rms(a) * gf_ref[...]).astype(o_ref.dtype)


def _post(x2d, ret2d, mo2d, gr2d, gm2d, bg_r, bg_m, w_ret_o, w_moba_o, w_out, g2, w_up, w_down, gf,
          *, tm):
    m, d = x2d.shape
    dff = w_up.shape[1]
    row = lambda width: pl.BlockSpec((tm, width), lambda i: (i, 0))
    const = lambda shape: pl.BlockSpec(shape, lambda i: (0, 0), pipeline_mode=pl.Buffered(1))
    return pl.pallas_call(
        functools.partial(_post_kernel, ff_chunk=1024),
        grid=(m // tm,),
        in_specs=[
            row(d), row(RET_V_W), row(MOBA_W), row(d), row(d),
            const((1, d)), const((1, d)),
            const((RET_V_W, d)), const((MOBA_W, d)), const((d, d)),
            const((1, d)), const((d, dff)), const((dff, d)), const((1, d)),
        ],
        out_specs=row(d),
        out_shape=jax.ShapeDtypeStruct((m, d), x2d.dtype),
        compiler_params=pltpu.CompilerParams(
            dimension_semantics=("parallel",),
            vmem_limit_bytes=VMEM_LIMIT_BYTES),
        name="merge_mlp",
    )(x2d, ret2d, mo2d, gr2d, gm2d, bg_r, bg_m, w_ret_o, w_moba_o, w_out, g2, w_up, w_down, gf)


def kernel(x, norm1_g, w_in, b_gate, w_ret_o, w_moba_o, w_out, norm2_g, w_up, w_down, normf_g):
    b, t, d = x.shape
    assert norm1_g.shape[0] == 1, "single-layer block"
    m = b * t
    x2d = x.reshape(m, d)
    rq, rk, rv, rg, mq, mk, mv, gr, gm = _in_proj(x2d, norm1_g, w_in[0].astype(BF16), b=b, t=t, tm=512)
    ret = _retention(rq, rk, rv, rg)
    mo = _moba(mq, mk, mv)
    bg = b_gate[0].reshape(2, 1, d)
    out = _post(x2d, ret.reshape(m, RET_V_W), mo.reshape(m, MOBA_W), gr, gm, bg[0], bg[1],
                w_ret_o[0].astype(BF16), w_moba_o[0].astype(BF16), w_out[0].astype(BF16),
                norm2_g, w_up[0].astype(BF16), w_down[0].astype(BF16), normf_g.reshape(1, d),
                tm=512)
    return out.reshape(b, t, d)
```

```python
import functools

import numpy as np
import jax
import jax.numpy as jnp
from jax import lax
from jax.experimental import pallas as pl
from jax.experimental.pallas import tpu as pltpu

RET_HEADS = 4
RET_QK_DIM = 128
RET_V_DIM = 256
MOBA_HEADS = 8
MOBA_HEAD_DIM = 64
MOBA_BLOCK = 256
MOBA_TOPK = 3
NORM_EPS = 1e-6
MASK_VALUE = -1e30

RET_QK_W = RET_HEADS * RET_QK_DIM
RET_V_W = RET_HEADS * RET_V_DIM
MOBA_W = MOBA_HEADS * MOBA_HEAD_DIM

RET_KERNEL_CHUNK = 256
RET_STEP_HEADS = 4
LANES = 128
SUBLANES = 8
MOBA_STEPS = MOBA_W // LANES
MOBA_STEP_HEADS = LANES // MOBA_HEAD_DIM
MOBA_HEAD_FEATS = 16
PROJ_CHUNK = 512
ROW_SPLITS = 2
ONES_ROWS = 16
MOBA_BOUND_SLACK = 1.0 + 2.0 ** -6
MOBA_DENOM_FLOOR = 1e-30
VMEM_LIMIT_BYTES = 56 * 1024 * 1024

F32 = jnp.float32
BF16 = jnp.bfloat16


def _dot(a, b):
    return jnp.dot(a, b, preferred_element_type=F32)


def _rms(x):
    return x * lax.rsqrt(jnp.mean(x * x, axis=-1, keepdims=True) + NORM_EPS)


def _sigmoid(x):
    return 1.0 / (1.0 + jnp.exp(-x))


def _in_proj_layout(d):
    return (("ret_q", RET_HEADS, RET_QK_DIM), ("ret_k", RET_HEADS, RET_QK_DIM),
            ("ret_v", RET_HEADS, RET_V_DIM), ("ret_g", RET_HEADS, RET_V_DIM),
            ("moba_q", MOBA_STEPS, LANES), ("moba_k", MOBA_STEPS, LANES), ("moba_v", MOBA_STEPS, LANES),
            ("gate_ret", 1, d), ("gate_moba", 1, d))


def _in_proj_kernel(x_ref, g_ref, w_ref, dq_ref, dk_ref, *out_refs, layout):
    rs = x_ref.shape[0] // ROW_SPLITS
    groups = [slice(i * rs, (i + 1) * rs) for i in range(ROW_SPLITS)]
    hs = [(_rms(x_ref[r, :]) * g_ref[...]).astype(BF16) for r in groups]
    epilogue = {
        "ret_q": lambda z, r, c: z * dq_ref[r, c:c + PROJ_CHUNK],
        "ret_k": lambda z, r, c: z * dk_ref[r, c:c + PROJ_CHUNK],
    }
    off = 0
    for ref, (name, heads, width) in zip(out_refs, layout):
        for c in range(0, heads * width, PROJ_CHUNK):
            for h, r in zip(hs, groups):
                zc = _dot(h, w_ref[:, off + c:off + c + PROJ_CHUNK])
                if name in epilogue:
                    zc = epilogue[name](zc, r, c)
                zc = zc.astype(BF16)
                if heads == 1:
                    ref[r, c:c + PROJ_CHUNK] = zc
                else:
                    for j in range(PROJ_CHUNK // width):
                        ref[0, c // width + j, r, :] = zc[:, j * width:(j + 1) * width]
        off += heads * width


def _retention_decay_tables(rows, chunk):
    log_g = np.log1p(-np.exp2(-5.0 - np.arange(RET_HEADS, dtype=np.float64)))
    steps = (np.arange(rows) % chunk + 1.0)[:, None, None] * log_g[None, :, None]
    expand = lambda a: jnp.asarray(np.broadcast_to(a, (rows, RET_HEADS, RET_QK_DIM)).reshape(rows, RET_QK_W)
                                   .astype(np.float32))
    return expand(np.exp(steps)), expand(np.exp(-steps) * RET_QK_DIM ** -0.5)


def _in_proj(x2d, g, w_bf16, *, b, t, tm):
    m, d = x2d.shape
    layout = _in_proj_layout(d)
    assert t % tm == 0 and tm % RET_KERNEL_CHUNK == 0
    assert w_bf16.shape[1] == sum(h * w for _, h, w in layout)
    tiles = t // tm
    dq, dk = _retention_decay_tables(tm, RET_KERNEL_CHUNK)
    out_specs, out_shapes = [], []
    for _, heads, width in layout:
        assert (heads * width) % PROJ_CHUNK == 0 and PROJ_CHUNK % width == 0 or heads == 1
        if heads == 1:
            out_specs.append(pl.BlockSpec((tm, width), lambda i: (i, 0)))
            out_shapes.append(jax.ShapeDtypeStruct((m, width), BF16))
        else:
            out_specs.append(pl.BlockSpec((1, heads, tm, width), lambda i: (i // tiles, 0, i % tiles, 0)))
            out_shapes.append(jax.ShapeDtypeStruct((b, heads, t, width), BF16))
    const = lambda a: pl.BlockSpec(a.shape, lambda i: (0, 0), pipeline_mode=pl.Buffered(1))
    return pl.pallas_call(
        functools.partial(_in_proj_kernel, layout=layout),
        grid=(m // tm,),
        in_specs=[
            pl.BlockSpec((tm, d), lambda i: (i, 0)),
            pl.BlockSpec((1, d), lambda i: (0, 0)),
            const(w_bf16), const(dq), const(dk),
        ],
        out_specs=out_specs,
        out_shape=out_shapes,
        compiler_params=pltpu.CompilerParams(
            dimension_semantics=("parallel",),
            vmem_limit_bytes=VMEM_LIMIT_BYTES),
        name="in_proj",
    )(x2d, g, w_bf16, dq, dk)


def _retention_kernel(cd_ref, q_ref, k_ref, v_ref, g_ref, o_ref, *, chunk):
    heads, t = q_ref.shape[1], q_ref.shape[2]
    dv = v_ref.shape[3]
    row = lax.broadcasted_iota(jnp.int32, (chunk, chunk), 0)
    col = lax.broadcasted_iota(jnp.int32, (chunk, chunk), 1)
    causal = col <= row
    cds = [cd_ref[pl.program_id(1) * heads + h] for h in range(heads)]
    states = [jnp.zeros((q_ref.shape[3], dv), F32) for _ in range(heads)]
    for n in range(t // chunk):
        rows = slice(n * chunk, (n + 1) * chunk)
        for h in range(heads):
            qc = q_ref[0, h, rows, :]
            kc = k_ref[0, h, rows, :]
            vc = v_ref[0, h, rows, :]
            kc_t = kc.T
            scores = jnp.where(causal, _dot(qc, kc_t), 0.0).astype(BF16)
            o = _dot(scores, vc) + _dot(qc, states[h].astype(BF16))
            half_g = 0.5 * g_ref[0, h, rows, :].astype(F32)
            swish = half_g * (1.0 + jnp.tanh(half_g))
            o_ref[0, rows, h * dv:(h + 1) * dv] = (_rms(o) * swish).astype(o_ref.dtype)
            states[h] = cds[h] * (states[h] + _dot(kc_t, vc))


def _retention(q, k, v, g):
    b, heads, t, _ = q.shape
    c = RET_KERNEL_CHUNK
    hs = RET_STEP_HEADS
    log_g = np.log1p(-np.exp2(-5.0 - np.arange(RET_HEADS, dtype=np.float64)))
    cd = jnp.asarray(np.exp(log_g * c).astype(np.float32))
    head_spec = lambda width: pl.BlockSpec((1, hs, t, width), lambda i, h: (i, h, 0, 0))
    return pl.pallas_call(
        functools.partial(_retention_kernel, chunk=c),
        grid=(b, heads // hs),
        in_specs=[
            pl.BlockSpec(memory_space=pltpu.SMEM),
            head_spec(RET_QK_DIM), head_spec(RET_QK_DIM), head_spec(RET_V_DIM), head_spec(RET_V_DIM),
        ],
        out_specs=pl.BlockSpec((1, t, hs * RET_V_DIM), lambda i, h: (i, 0, h)),
        out_shape=jax.ShapeDtypeStruct((b, t, RET_V_W), BF16),
        compiler_params=pltpu.CompilerParams(
            dimension_semantics=("parallel", "arbitrary"),
            vmem_limit_bytes=VMEM_LIMIT_BYTES),
        name="retention",
    )(cd, q, k, v, g)


def _moba_key_features(t, blk):
    s = np.arange(t)
    feats = np.zeros((t, LANES), np.float32)
    nb = t // blk
    for f0 in range(0, MOBA_STEP_HEADS * MOBA_HEAD_FEATS, MOBA_HEAD_FEATS):
        feats[s, f0 + s // blk] = 1.0
        feats[:, f0 + nb] = s % blk
        feats[:, f0 + nb + 1] = s // blk
        feats[:, f0 + nb + 2] = 1.0
    return jnp.asarray(feats, dtype=BF16)


def _moba_kernel(slope_ref, q_ref, k_ref, v_ref, kx_ref, o_ref, *maybe_d_ref, blk, topk, head_dim, exact):
    (d_ref,) = maybe_d_ref or (None,)
    t = q_ref.shape[2]
    nb = t // blk
    width = q_ref.shape[3]
    heads_per_step = width // head_dim
    scale = head_dim ** -0.5
    q_all = q_ref[0, 0]
    k_all = k_ref[0, 0]
    k_aug = jnp.concatenate([k_all, kx_ref[...]], axis=1)
    v_t = v_ref[0, 0].astype(F32).T.astype(BF16)
    lane = lax.broadcasted_iota(jnp.int32, (1, width), 1)
    in_head = [(lane >= h * head_dim) & (lane < (h + 1) * head_dim) for h in range(heads_per_step)]

    k_mean = jnp.mean(k_all.astype(F32).reshape(nb, blk, width), axis=1)
    km_hi = k_mean.astype(BF16).astype(F32)
    km_mid = (k_mean - km_hi).astype(BF16).astype(F32)
    km_lo = k_mean - km_hi - km_mid
    pieces = [jnp.where(in_head[h], part, 0.0) for h in range(heads_per_step) for part in (km_hi, km_mid, km_lo)]
    q_t = q_all.astype(F32).T
    gates = _dot(jnp.concatenate(pieces, axis=0).astype(BF16), q_t.astype(BF16))

    key_i = lax.broadcasted_iota(jnp.int32, (blk, blk), 0)
    qry_i = lax.broadcasted_iota(jnp.int32, (blk, blk), 1)
    causal_t = key_i <= qry_i

    g_row = lax.broadcasted_iota(jnp.int32, (nb, t), 0)
    g_qblk = lax.broadcasted_iota(jnp.int32, (nb, t), 1) // blk
    eligible = g_row < g_qblk
    own = g_row == g_qblk
    x_row = lax.broadcasted_iota(jnp.int32, (MOBA_HEAD_FEATS - nb, t), 0)

    q_row = lax.broadcasted_iota(jnp.int32, (width, 1), 0)
    k_f32 = k_all.astype(F32)
    k_sq_max = jnp.max(k_f32 * k_f32, axis=0, keepdims=True)
    t_pos = lax.broadcasted_iota(jnp.int32, (1, t), 1).astype(F32)
    q_scaled_t = q_t * scale
    feat_pad = jnp.zeros((MOBA_HEAD_FEATS, t), F32)
    tail_pad = jnp.zeros((LANES - heads_per_step * MOBA_HEAD_FEATS, t), F32)
    ones_rows = jnp.ones((ONES_ROWS, t), BF16)
    q_augs_t, v_exts = [], []
    for head in range(heads_per_step):
        slope = slope_ref[pl.program_id(1) * heads_per_step + head]
        g0 = 3 * head * nb
        gate = gates[g0:g0 + nb] + gates[g0 + nb:g0 + 2 * nb] + gates[g0 + 2 * nb:g0 + 3 * nb]
        gate = jnp.where(eligible, gate, -jnp.inf)
        rank = jnp.zeros((nb, t), jnp.int32)
        for m in range(nb):
            g_m = gate[m:m + 1, :]
            beats = (g_m > gate) | ((g_m == gate) & (m < g_row))
            rank = rank + beats.astype(jnp.int32)
        selected = eligible & (rank < topk) & (jnp.abs(gate) < jnp.inf)
        sel_bias_t = jnp.where(selected | own, 0.0, MASK_VALUE)
        in_head_rows = (q_row >= head * head_dim) & (q_row < (head + 1) * head_dim)
        q_head = jnp.where(in_head_rows, q_t, 0.0)
        q_sq = jnp.sum(q_head * q_head, axis=0, keepdims=True)
        k_sq = jnp.sum(jnp.where(in_head[head], k_sq_max, 0.0), axis=1, keepdims=True)
        bound = (scale * MOBA_BOUND_SLACK) * jnp.sqrt(q_sq * k_sq)
        shift = -(slope * t_pos + bound)
        slope_rows = jnp.where(x_row == 0, slope, jnp.where(x_row == 1, slope * float(blk),
                                                           jnp.where(x_row == 2, shift, 0.0)))
        parts = [jnp.where(in_head_rows, q_scaled_t, 0.0)]
        parts += [feat_pad] * head + [sel_bias_t, slope_rows] + [feat_pad] * (heads_per_step - 1 - head)
        parts.append(tail_pad)
        q_augs_t.append(jnp.concatenate(parts, axis=0).astype(BF16))
        v_exts.append(jnp.concatenate([v_t[head * head_dim:(head + 1) * head_dim, :], ones_rows], axis=0))

    def scores(qb, head):
        nk = (qb + 1) * blk
        return _dot(k_aug[:nk], q_augs_t[head][:, qb * blk:nk])

    def attend(qb, head, s_t):
        past = qb * blk
        diag = jnp.where(causal_t, s_t[past:], MASK_VALUE)
        m_col = jnp.max(diag, axis=0, keepdims=True)
        if qb:
            m_col = jnp.maximum(m_col, jnp.max(s_t[:past], axis=0, keepdims=True))
        v_e = v_exts[head]
        acc = _dot(v_e[:, past:past + blk], jnp.exp((diag - m_col).astype(BF16)))
        if qb:
            acc = acc + _dot(v_e[:, :past], jnp.exp((s_t[:past] - m_col).astype(BF16)))
        denom = acc[head_dim:head_dim + 1]
        return acc[:head_dim] / denom, denom

    def attend_bounded(qb, head, s_t):
        past = qb * blk
        v_e = v_exts[head]
        acc = _dot(v_e[:, past:past + blk], jnp.exp(jnp.where(causal_t, s_t[past:], MASK_VALUE)).astype(BF16))
        if qb:
            acc = acc + _dot(v_e[:, :past], jnp.exp(s_t[:past]).astype(BF16))
        denom = acc[head_dim:head_dim + 1]
        return acc[:head_dim] / denom, denom

    def run(attend_fn):
        heads = range(heads_per_step)
        order = list(reversed(range(nb)))
        denom_min = None
        pending = [scores(order[0], h) for h in heads]
        for i, qb in enumerate(order):
            nxt = [scores(order[i + 1], h) for h in heads] if i + 1 < nb else None
            block_out_t = []
            for h in heads:
                out_t, denom = attend_fn(qb, h, pending[h])
                denom_min = denom if denom_min is None else jnp.minimum(denom_min, denom)
                block_out_t.append(out_t)
            o_ref[0, qb * blk:(qb + 1) * blk, :] = jnp.concatenate(block_out_t, axis=0).T.astype(o_ref.dtype)
            pending = nxt
        return denom_min

    if exact:
        run(attend)
    else:
        denom_min = run(attend_bounded)
        d_ref[0, 0] = jnp.broadcast_to(jnp.min(denom_min, axis=1, keepdims=True), d_ref.shape[2:])


def _moba(q, k, v, *, exact):
    b, steps, t, width = q.shape
    assert t % MOBA_BLOCK == 0 and t // MOBA_BLOCK + 3 <= MOBA_HEAD_FEATS
    slopes = jnp.asarray(np.exp2(-8.0 / MOBA_HEADS * np.arange(1, MOBA_HEADS + 1)).astype(np.float32))
    spec = pl.BlockSpec((1, 1, t, width), lambda i, h: (i, h, 0, 0))
    out_specs = [pl.BlockSpec((1, t, width), lambda i, h: (i, 0, h))]
    out_shapes = [jax.ShapeDtypeStruct((b, t, MOBA_W), BF16)]
    if not exact:
        out_specs.append(pl.BlockSpec((1, 1, SUBLANES, LANES), lambda i, h: (i, h, 0, 0)))
        out_shapes.append(jax.ShapeDtypeStruct((b, steps, SUBLANES, LANES), F32))
    outs = pl.pallas_call(
        functools.partial(_moba_kernel, blk=MOBA_BLOCK, topk=MOBA_TOPK, head_dim=MOBA_HEAD_DIM, exact=exact),
        grid=(b, steps),
        in_specs=[pl.BlockSpec(memory_space=pltpu.SMEM), spec, spec, spec,
                  pl.BlockSpec((t, LANES), lambda i, h: (0, 0))],
        out_specs=out_specs,
        out_shape=out_shapes,
        compiler_params=pltpu.CompilerParams(
            dimension_semantics=("parallel", "arbitrary"),
            vmem_limit_bytes=VMEM_LIMIT_BYTES),
        name="moba_exact" if exact else "moba",
    )(slopes, q, k, v, _moba_key_features(t, MOBA_BLOCK))
    return outs[0] if exact else tuple(outs)


def _moba_guarded(q, k, v):
    out, denoms = _moba(q, k, v, exact=False)
    return lax.cond(jnp.min(denoms) > MOBA_DENOM_FLOOR, lambda: out, lambda: _moba(q, k, v, exact=True))


def _post_kernel(x_ref, ret_ref, mo_ref, gr_ref, gm_ref, bgr_ref, bgm_ref, wro_ref, wmo_ref,
                 wout_ref, g2_ref, wup_ref, wdn_ref, gf_ref, o_ref, *, ff_chunk):
    rs = x_ref.shape[0] // ROW_SPLITS
    groups = [slice(i * rs, (i + 1) * rs) for i in range(ROW_SPLITS)]
    y_ret = [_dot(ret_ref[r, :], wro_ref[...]) for r in groups]
    y_moba = [_dot(mo_ref[r, :], wmo_ref[...]) for r in groups]
    merged = []
    for i, r in enumerate(groups):
        gate_r = _sigmoid(gr_ref[r, :].astype(F32) + bgr_ref[...])
        gate_m = _sigmoid(gm_ref[r, :].astype(F32) + bgm_ref[...])
        merged.append((gate_r * y_ret[i] + gate_m * y_moba[i]).astype(BF16))
    x1 = [x_ref[r, :] + _dot(merged[i], wout_ref[...]) for i, r in enumerate(groups)]
    h2 = [(_rms(x) * g2_ref[...]).astype(BF16) for x in x1]
    acc = list(x1)
    for c in range(0, wup_ref.shape[1], ff_chunk):
        u = [_dot(h, wup_ref[:, c:c + ff_chunk]) for h in h2]
        u = [jnp.square(jnp.maximum(v, 0.0)).astype(BF16) for v in u]
        acc = [a + _dot(v, wdn_ref[c:c + ff_chunk, :]) for a, v in zip(acc, u)]
    for a, r in zip(acc, groups):
        o_ref[r, :] = (_rms(a) * gf_ref[...]).astype(o_ref.dtype)


def _post(x2d, ret2d, mo2d, gr2d, gm2d, bg_r, bg_m, w_ret_o, w_moba_o, w_out, g2, w_up, w_down, gf,
          *, tm):
    m, d = x2d.shape
    dff = w_up.shape[1]
    row = lambda width: pl.BlockSpec((tm, width), lambda i: (i, 0))
    const = lambda shape: pl.BlockSpec(shape, lambda i: (0, 0), pipeline_mode=pl.Buffered(1))
    return pl.pallas_call(
        functools.partial(_post_kernel, ff_chunk=1024),
        grid=(m // tm,),
        in_specs=[
            row(d), row(RET_V_W), row(MOBA_W), row(d), row(d),
            const((1, d)), const((1, d)),
            const((RET_V_W, d)), const((MOBA_W, d)), const((d, d)),
            const((1, d)), const((d, dff)), const((dff, d)), const((1, d)),
        ],
        out_specs=row(d),
        out_shape=jax.ShapeDtypeStruct((m, d), x2d.dtype),
        compiler_params=pltpu.CompilerParams(
            dimension_semantics=("parallel",),
            vmem_limit_bytes=VMEM_LIMIT_BYTES),
        name="merge_mlp",
    )(x2d, ret2d, mo2d, gr2d, gm2d, bg_r, bg_m, w_ret_o, w_moba_o, w_out, g2, w_up, w_down, gf)


def kernel(x, norm1_g, w_in, b_gate, w_ret_o, w_moba_o, w_out, norm2_g, w_up, w_down, normf_g):
    b, t, d = x.shape
    assert norm1_g.shape[0] == 1, "single-layer block"
    m = b * t
    x2d = x.reshape(m, d)
    rq, rk, rv, rg, mq, mk, mv, gr, gm = _in_proj(x2d, norm1_g, w_in[0].astype(BF16), b=b, t=t, tm=512)
    ret = _retention(rq, rk, rv, rg)
    mo = _moba_guarded(mq, mk, mv)
    bg = b_gate[0].reshape(2, 1, d)
    out = _post(x2d, ret.reshape(m, RET_V_W), mo.reshape(m, MOBA_W), gr, gm, bg[0], bg[1],
                w_ret_o[0].astype(BF16), w_moba_o[0].astype(BF16), w_out[0].astype(BF16),
                norm2_g, w_up[0].astype(BF16), w_down[0].astype(BF16), normf_g.reshape(1, d),
                tm=512)
    return out.reshape(b, t, d)
```

```python
import functools

import numpy as np
import jax
import jax.numpy as jnp
from jax import lax
from jax.experimental import pallas as pl
from jax.experimental.pallas import tpu as pltpu

RET_HEADS = 4
RET_QK_DIM = 128
RET_V_DIM = 256
MOBA_HEADS = 8
MOBA_HEAD_DIM = 64
MOBA_BLOCK = 256
MOBA_TOPK = 3
NORM_EPS = 1e-6
MASK_VALUE = -1e30

RET_QK_W = RET_HEADS * RET_QK_DIM
RET_V_W = RET_HEADS * RET_V_DIM
MOBA_W = MOBA_HEADS * MOBA_HEAD_DIM

RET_KERNEL_CHUNK = 256
RET_STEP_HEADS = 4
LANES = 128
SUBLANES = 8
MOBA_STEPS = MOBA_W // LANES
MOBA_STEP_HEADS = LANES // MOBA_HEAD_DIM
MOBA_HEAD_FEATS = 16
PROJ_CHUNK = 512
ROW_SPLITS = 2
MLP_HIDDEN_CHUNK = 512
IN_PROJ_ROWS = 512
MERGE_ROWS = 512
ONES_ROWS = 16
MOBA_BOUND_SLACK = 1.0 + 2.0 ** -6
MOBA_DENOM_FLOOR = 1e-30
VMEM_LIMIT_BYTES = 56 * 1024 * 1024

F32 = jnp.float32
BF16 = jnp.bfloat16


def _dot(a, b):
    return jnp.dot(a, b, preferred_element_type=F32)


def _rms(x):
    return x * lax.rsqrt(jnp.mean(x * x, axis=-1, keepdims=True) + NORM_EPS)


def _sigmoid(x):
    return 1.0 / (1.0 + jnp.exp(-x))


def _in_proj_layout(d):
    return (("ret_q", RET_HEADS, RET_QK_DIM), ("ret_k", RET_HEADS, RET_QK_DIM),
            ("ret_v", RET_HEADS, RET_V_DIM), ("ret_g", RET_HEADS, RET_V_DIM),
            ("moba_q", MOBA_STEPS, LANES), ("moba_k", MOBA_STEPS, LANES), ("moba_v", MOBA_STEPS, LANES),
            ("gate_ret", 1, d), ("gate_moba", 1, d))


def _in_proj_kernel(x_ref, g_ref, w_ref, dq_ref, dk_ref, *out_refs, layout):
    rs = x_ref.shape[0] // ROW_SPLITS
    groups = [slice(i * rs, (i + 1) * rs) for i in range(ROW_SPLITS)]
    hs = [(_rms(x_ref[r, :]) * g_ref[...]).astype(BF16) for r in groups]
    epilogue = {
        "ret_q": lambda z, r, c: z * dq_ref[r, c:c + PROJ_CHUNK],
        "ret_k": lambda z, r, c: z * dk_ref[r, c:c + PROJ_CHUNK],
    }
    off = 0
    for ref, (name, heads, width) in zip(out_refs, layout):
        for c in range(0, heads * width, PROJ_CHUNK):
            for h, r in zip(hs, groups):
                zc = _dot(h, w_ref[:, off + c:off + c + PROJ_CHUNK])
                if name in epilogue:
                    zc = epilogue[name](zc, r, c)
                zc = zc.astype(BF16)
                if heads == 1:
                    ref[r, c:c + PROJ_CHUNK] = zc
                else:
                    for j in range(PROJ_CHUNK // width):
                        ref[0, c // width + j, r, :] = zc[:, j * width:(j + 1) * width]
        off += heads * width


def _retention_decay_tables(rows, chunk):
    log_g = np.log1p(-np.exp2(-5.0 - np.arange(RET_HEADS, dtype=np.float64)))
    steps = (np.arange(rows) % chunk + 1.0)[:, None, None] * log_g[None, :, None]
    expand = lambda a: jnp.asarray(np.broadcast_to(a, (rows, RET_HEADS, RET_QK_DIM)).reshape(rows, RET_QK_W)
                                   .astype(np.float32))
    return expand(np.exp(steps)), expand(np.exp(-steps) * RET_QK_DIM ** -0.5)


def _in_proj(x2d, g, w_bf16, *, b, t, tm):
    m, d = x2d.shape
    layout = _in_proj_layout(d)
    assert t % tm == 0 and tm % RET_KERNEL_CHUNK == 0
    assert w_bf16.shape[1] == sum(h * w for _, h, w in layout)
    tiles = t // tm
    dq, dk = _retention_decay_tables(tm, RET_KERNEL_CHUNK)
    out_specs, out_shapes = [], []
    for _, heads, width in layout:
        assert (heads * width) % PROJ_CHUNK == 0 and PROJ_CHUNK % width == 0 or heads == 1
        if heads == 1:
            out_specs.append(pl.BlockSpec((tm, width), lambda i: (i, 0)))
            out_shapes.append(jax.ShapeDtypeStruct((m, width), BF16))
        else:
            out_specs.append(pl.BlockSpec((1, heads, tm, width), lambda i: (i // tiles, 0, i % tiles, 0)))
            out_shapes.append(jax.ShapeDtypeStruct((b, heads, t, width), BF16))
    const = lambda a: pl.BlockSpec(a.shape, lambda i: (0, 0), pipeline_mode=pl.Buffered(1))
    return pl.pallas_call(
        functools.partial(_in_proj_kernel, layout=layout),
        grid=(m // tm,),
        in_specs=[
            pl.BlockSpec((tm, d), lambda i: (i, 0)),
            pl.BlockSpec((1, d), lambda i: (0, 0)),
            const(w_bf16), const(dq), const(dk),
        ],
        out_specs=out_specs,
        out_shape=out_shapes,
        compiler_params=pltpu.CompilerParams(
            dimension_semantics=("parallel",),
            vmem_limit_bytes=VMEM_LIMIT_BYTES),
        name="in_proj",
    )(x2d, g, w_bf16, dq, dk)


def _retention_kernel(cd_ref, q_ref, k_ref, v_ref, g_ref, o_ref, *, chunk):
    heads, t = q_ref.shape[1], q_ref.shape[2]
    dv = v_ref.shape[3]
    row = lax.broadcasted_iota(jnp.int32, (chunk, chunk), 0)
    col = lax.broadcasted_iota(jnp.int32, (chunk, chunk), 1)
    causal = col <= row
    cds = [cd_ref[pl.program_id(1) * heads + h] for h in range(heads)]
    states = [jnp.zeros((q_ref.shape[3], dv), F32) for _ in range(heads)]
    for n in range(t // chunk):
        rows = slice(n * chunk, (n + 1) * chunk)
        for h in range(heads):
            qc = q_ref[0, h, rows, :]
            kc = k_ref[0, h, rows, :]
            vc = v_ref[0, h, rows, :]
            kc_t = kc.T
            scores = jnp.where(causal, _dot(qc, kc_t), 0.0).astype(BF16)
            o = _dot(scores, vc) + _dot(qc, states[h].astype(BF16))
            half_g = 0.5 * g_ref[0, h, rows, :].astype(F32)
            swish = half_g * (1.0 + jnp.tanh(half_g))
            o_ref[0, rows, h * dv:(h + 1) * dv] = (_rms(o) * swish).astype(o_ref.dtype)
            states[h] = cds[h] * (states[h] + _dot(kc_t, vc))


def _retention(q, k, v, g):
    b, heads, t, _ = q.shape
    c = RET_KERNEL_CHUNK
    hs = RET_STEP_HEADS
    log_g = np.log1p(-np.exp2(-5.0 - np.arange(RET_HEADS, dtype=np.float64)))
    cd = jnp.asarray(np.exp(log_g * c).astype(np.float32))
    head_spec = lambda width: pl.BlockSpec((1, hs, t, width), lambda i, h: (i, h, 0, 0))
    return pl.pallas_call(
        functools.partial(_retention_kernel, chunk=c),
        grid=(b, heads // hs),
        in_specs=[
            pl.BlockSpec(memory_space=pltpu.SMEM),
            head_spec(RET_QK_DIM), head_spec(RET_QK_DIM), head_spec(RET_V_DIM), head_spec(RET_V_DIM),
        ],
        out_specs=pl.BlockSpec((1, t, hs * RET_V_DIM), lambda i, h: (i, 0, h)),
        out_shape=jax.ShapeDtypeStruct((b, t, RET_V_W), BF16),
        compiler_params=pltpu.CompilerParams(
            dimension_semantics=("parallel", "arbitrary"),
            vmem_limit_bytes=VMEM_LIMIT_BYTES),
        name="retention",
    )(cd, q, k, v, g)


def _moba_key_features(t, blk):
    s = np.arange(t)
    feats = np.zeros((t, LANES), np.float32)
    nb = t // blk
    for f0 in range(0, MOBA_STEP_HEADS * MOBA_HEAD_FEATS, MOBA_HEAD_FEATS):
        feats[s, f0 + s // blk] = 1.0
        feats[:, f0 + nb] = s % blk
        feats[:, f0 + nb + 1] = s // blk
        feats[:, f0 + nb + 2] = 1.0
    return jnp.asarray(feats, dtype=BF16)


def _moba_kernel(slope_ref, q_ref, k_ref, v_ref, kx_ref, o_ref, *maybe_d_ref, blk, topk, head_dim, exact):
    (d_ref,) = maybe_d_ref or (None,)
    t = q_ref.shape[2]
    nb = t // blk
    width = q_ref.shape[3]
    heads_per_step = width // head_dim
    scale = head_dim ** -0.5
    q_all = q_ref[0, 0]
    k_all = k_ref[0, 0]
    k_aug = jnp.concatenate([k_all, kx_ref[...]], axis=1)
    v_t = v_ref[0, 0].astype(F32).T.astype(BF16)
    lane = lax.broadcasted_iota(jnp.int32, (1, width), 1)
    in_head = [(lane >= h * head_dim) & (lane < (h + 1) * head_dim) for h in range(heads_per_step)]

    k_mean = jnp.mean(k_all.astype(F32).reshape(nb, blk, width), axis=1)
    km_hi = k_mean.astype(BF16).astype(F32)
    km_mid = (k_mean - km_hi).astype(BF16).astype(F32)
    km_lo = k_mean - km_hi - km_mid
    pieces = [jnp.where(in_head[h], part, 0.0) for h in range(heads_per_step) for part in (km_hi, km_mid, km_lo)]
    q_t = q_all.astype(F32).T
    gates = _dot(jnp.concatenate(pieces, axis=0).astype(BF16), q_t.astype(BF16))

    key_i = lax.broadcasted_iota(jnp.int32, (blk, blk), 0)
    qry_i = lax.broadcasted_iota(jnp.int32, (blk, blk), 1)
    causal_t = key_i <= qry_i

    g_row = lax.broadcasted_iota(jnp.int32, (nb, t), 0)
    g_qblk = lax.broadcasted_iota(jnp.int32, (nb, t), 1) // blk
    eligible = g_row < g_qblk
    own = g_row == g_qblk
    x_row = lax.broadcasted_iota(jnp.int32, (MOBA_HEAD_FEATS - nb, t), 0)

    q_row = lax.broadcasted_iota(jnp.int32, (width, 1), 0)
    k_f32 = k_all.astype(F32)
    k_sq_max = jnp.max(k_f32 * k_f32, axis=0, keepdims=True)
    t_pos = lax.broadcasted_iota(jnp.int32, (1, t), 1).astype(F32)
    q_scaled_t = q_t * scale
    feat_pad = jnp.zeros((MOBA_HEAD_FEATS, t), F32)
    tail_pad = jnp.zeros((LANES - heads_per_step * MOBA_HEAD_FEATS, t), F32)
    ones_rows = jnp.ones((ONES_ROWS, t), BF16)
    q_augs_t, v_exts = [], []
    for head in range(heads_per_step):
        slope = slope_ref[pl.program_id(1) * heads_per_step + head]
        g0 = 3 * head * nb
        gate = gates[g0:g0 + nb] + gates[g0 + nb:g0 + 2 * nb] + gates[g0 + 2 * nb:g0 + 3 * nb]
        gate = jnp.where(eligible, gate, -jnp.inf)
        rank = jnp.zeros((nb, t), jnp.int32)
        for m in range(nb):
            g_m = gate[m:m + 1, :]
            beats = (g_m > gate) | ((g_m == gate) & (m < g_row))
            rank = rank + beats.astype(jnp.int32)
        selected = eligible & (rank < topk) & (jnp.abs(gate) < jnp.inf)
        sel_bias_t = jnp.where(selected | own, 0.0, MASK_VALUE)
        in_head_rows = (q_row >= head * head_dim) & (q_row < (head + 1) * head_dim)
        q_head = jnp.where(in_head_rows, q_t, 0.0)
        q_sq = jnp.sum(q_head * q_head, axis=0, keepdims=True)
        k_sq = jnp.sum(jnp.where(in_head[head], k_sq_max, 0.0), axis=1, keepdims=True)
        bound = (scale * MOBA_BOUND_SLACK) * jnp.sqrt(q_sq * k_sq)
        shift = -(slope * t_pos + bound)
        slope_rows = jnp.where(x_row == 0, slope, jnp.where(x_row == 1, slope * float(blk),
                                                           jnp.where(x_row == 2, shift, 0.0)))
        parts = [jnp.where(in_head_rows, q_scaled_t, 0.0)]
        parts += [feat_pad] * head + [sel_bias_t, slope_rows] + [feat_pad] * (heads_per_step - 1 - head)
        parts.append(tail_pad)
        q_augs_t.append(jnp.concatenate(parts, axis=0).astype(BF16))
        v_exts.append(jnp.concatenate([v_t[head * head_dim:(head + 1) * head_dim, :], ones_rows], axis=0))

    def scores(qb, head):
        nk = (qb + 1) * blk
        return _dot(k_aug[:nk], q_augs_t[head][:, qb * blk:nk])

    def attend(qb, head, s_t):
        past = qb * blk
        diag = jnp.where(causal_t, s_t[past:], MASK_VALUE)
        m_col = jnp.max(diag, axis=0, keepdims=True)
        if qb:
            m_col = jnp.maximum(m_col, jnp.max(s_t[:past], axis=0, keepdims=True))
        v_e = v_exts[head]
        acc = _dot(v_e[:, past:past + blk], jnp.exp((diag - m_col).astype(BF16)))
        if qb:
            acc = acc + _dot(v_e[:, :past], jnp.exp((s_t[:past] - m_col).astype(BF16)))
        denom = acc[head_dim:head_dim + 1]
        return acc[:head_dim] / denom, denom

    def attend_bounded(qb, head, s_t):
        past = qb * blk
        v_e = v_exts[head]
        acc = _dot(v_e[:, past:past + blk], jnp.exp(jnp.where(causal_t, s_t[past:], MASK_VALUE)).astype(BF16))
        if qb:
            acc = acc + _dot(v_e[:, :past], jnp.exp(s_t[:past]).astype(BF16))
        denom = acc[head_dim:head_dim + 1]
        return acc[:head_dim] / denom, denom

    def run(attend_fn):
        heads = range(heads_per_step)
        order = list(reversed(range(nb)))
        denom_min = None
        pending = [scores(order[0], h) for h in heads]
        for i, qb in enumerate(order):
            nxt = [scores(order[i + 1], h) for h in heads] if i + 1 < nb else None
            block_out_t = []
            for h in heads:
                out_t, denom = attend_fn(qb, h, pending[h])
                denom_min = denom if denom_min is None else jnp.minimum(denom_min, denom)
                block_out_t.append(out_t)
            o_ref[0, qb * blk:(qb + 1) * blk, :] = jnp.concatenate(block_out_t, axis=0).T.astype(o_ref.dtype)
            pending = nxt
        return denom_min

    if exact:
        run(attend)
    else:
        denom_min = run(attend_bounded)
        d_ref[0, 0] = jnp.broadcast_to(jnp.min(denom_min, axis=1, keepdims=True), d_ref.shape[2:])


def _moba(q, k, v, *, exact):
    b, steps, t, width = q.shape
    assert t % MOBA_BLOCK == 0 and t // MOBA_BLOCK + 3 <= MOBA_HEAD_FEATS
    slopes = jnp.asarray(np.exp2(-8.0 / MOBA_HEADS * np.arange(1, MOBA_HEADS + 1)).astype(np.float32))
    spec = pl.BlockSpec((1, 1, t, width), lambda i, h: (i, h, 0, 0))
    out_specs = [pl.BlockSpec((1, t, width), lambda i, h: (i, 0, h))]
    out_shapes = [jax.ShapeDtypeStruct((b, t, MOBA_W), BF16)]
    if not exact:
        out_specs.append(pl.BlockSpec((1, 1, SUBLANES, LANES), lambda i, h: (i, h, 0, 0)))
        out_shapes.append(jax.ShapeDtypeStruct((b, steps, SUBLANES, LANES), F32))
    outs = pl.pallas_call(
        functools.partial(_moba_kernel, blk=MOBA_BLOCK, topk=MOBA_TOPK, head_dim=MOBA_HEAD_DIM, exact=exact),
        grid=(b, steps),
        in_specs=[pl.BlockSpec(memory_space=pltpu.SMEM), spec, spec, spec,
                  pl.BlockSpec((t, LANES), lambda i, h: (0, 0))],
        out_specs=out_specs,
        out_shape=out_shapes,
        compiler_params=pltpu.CompilerParams(
            dimension_semantics=("parallel", "arbitrary"),
            vmem_limit_bytes=VMEM_LIMIT_BYTES),
        name="moba_exact" if exact else "moba",
    )(slopes, q, k, v, _moba_key_features(t, MOBA_BLOCK))
    return outs[0] if exact else tuple(outs)


def _moba_guarded(q, k, v):
    out, denoms = _moba(q, k, v, exact=False)
    return lax.cond(jnp.min(denoms) > MOBA_DENOM_FLOOR, lambda: out, lambda: _moba(q, k, v, exact=True))


def _post_kernel(x_ref, ret_ref, mo_ref, gr_ref, gm_ref, bgr_ref, bgm_ref, wro_ref, wmo_ref,
                 wout_ref, g2_ref, wup_ref, wdn_ref, gf_ref, o_ref, *, ff_chunk):
    rs = x_ref.shape[0] // ROW_SPLITS
    groups = [slice(i * rs, (i + 1) * rs) for i in range(ROW_SPLITS)]
    y_ret = [_dot(ret_ref[r, :], wro_ref[...]) for r in groups]
    y_moba = [_dot(mo_ref[r, :], wmo_ref[...]) for r in groups]
    merged = []
    for i, r in enumerate(groups):
        gate_r = _sigmoid(gr_ref[r, :].astype(F32) + bgr_ref[...])
        gate_m = _sigmoid(gm_ref[r, :].astype(F32) + bgm_ref[...])
        merged.append((gate_r * y_ret[i] + gate_m * y_moba[i]).astype(BF16))
    x1 = [x_ref[r, :] + _dot(merged[i], wout_ref[...]) for i, r in enumerate(groups)]
    h2 = [(_rms(x) * g2_ref[...]).astype(BF16) for x in x1]
    acc = list(x1)
    for c in range(0, wup_ref.shape[1], ff_chunk):
        u = [_dot(h, wup_ref[:, c:c + ff_chunk]) for h in h2]
        u = [jnp.square(jnp.maximum(v, 0.0)).astype(BF16) for v in u]
        acc = [a + _dot(v, wdn_ref[c:c + ff_chunk, :]) for a, v in zip(acc, u)]
    for a, r in zip(acc, groups):
        o_ref[r, :] = (_rms(a) * gf_ref[...]).astype(o_ref.dtype)


def _post(x2d, ret2d, mo2d, gr2d, gm2d, bg_r, bg_m, w_ret_o, w_moba_o, w_out, g2, w_up, w_down, gf,
          *, tm):
    m, d = x2d.shape
    dff = w_up.shape[1]
    row = lambda width: pl.BlockSpec((tm, width), lambda i: (i, 0))
    const = lambda shape: pl.BlockSpec(shape, lambda i: (0, 0), pipeline_mode=pl.Buffered(1))
    return pl.pallas_call(
        functools.partial(_post_kernel, ff_chunk=MLP_HIDDEN_CHUNK),
        grid=(m // tm,),
        in_specs=[
            row(d), row(RET_V_W), row(MOBA_W), row(d), row(d),
            const((1, d)), const((1, d)),
            const((RET_V_W, d)), const((MOBA_W, d)), const((d, d)),
            const((1, d)), const((d, dff)), const((dff, d)), const((1, d)),
        ],
        out_specs=row(d),
        out_shape=jax.ShapeDtypeStruct((m, d), x2d.dtype),
        compiler_params=pltpu.CompilerParams(
            dimension_semantics=("parallel",),
            vmem_limit_bytes=VMEM_LIMIT_BYTES),
        name="merge_mlp",
    )(x2d, ret2d, mo2d, gr2d, gm2d, bg_r, bg_m, w_ret_o, w_moba_o, w_out, g2, w_up, w_down, gf)


def kernel(x, norm1_g, w_in, b_gate, w_ret_o, w_moba_o, w_out, norm2_g, w_up, w_down, normf_g):
    b, t, d = x.shape
    assert norm1_g.shape[0] == 1, "single-layer block"
    m = b * t
    x2d = x.reshape(m, d)
    rq, rk, rv, rg, mq, mk, mv, gr, gm = _in_proj(x2d, norm1_g, w_in[0].astype(BF16), b=b, t=t,
                                                  tm=IN_PROJ_ROWS)
    ret = _retention(rq, rk, rv, rg)
    mo = _moba_guarded(mq, mk, mv)
    bg = b_gate[0].reshape(2, 1, d)
    out = _post(x2d, ret.reshape(m, RET_V_W), mo.reshape(m, MOBA_W), gr, gm, bg[0], bg[1],
                w_ret_o[0].astype(BF16), w_moba_o[0].astype(BF16), w_out[0].astype(BF16),
                norm2_g, w_up[0].astype(BF16), w_down[0].astype(BF16), normf_g.reshape(1, d),
                tm=MERGE_ROWS)
    return out.reshape(b, t, d)
```

```python
import functools

import numpy as np
import jax
import jax.numpy as jnp
from jax import lax
from jax.experimental import pallas as pl
from jax.experimental.pallas import tpu as pltpu

RET_HEADS = 4
RET_QK_DIM = 128
RET_V_DIM = 256
MOBA_HEADS = 8
MOBA_HEAD_DIM = 64
MOBA_BLOCK = 256
MOBA_TOPK = 3
NORM_EPS = 1e-6
MASK_VALUE = -1e30

RET_QK_W = RET_HEADS * RET_QK_DIM
RET_V_W = RET_HEADS * RET_V_DIM
MOBA_W = MOBA_HEADS * MOBA_HEAD_DIM

RET_KERNEL_CHUNK = 256
RET_INPUT_BUFFERS = 3
LANES = 128
SUBLANES = 8
MOBA_STEPS = MOBA_W // LANES
MOBA_STEP_HEADS = LANES // MOBA_HEAD_DIM
MOBA_HEAD_FEATS = 16
PROJ_CHUNK = 512
ROW_SPLITS = 2
MLP_HIDDEN_CHUNK = 512
IN_PROJ_ROWS = 512
MERGE_ROWS = 512
ONES_ROWS = 16
MOBA_BOUND_SLACK = 1.0 + 2.0 ** -6
MOBA_DENOM_FLOOR = 1e-30
VMEM_LIMIT_BYTES = 56 * 1024 * 1024

F32 = jnp.float32
BF16 = jnp.bfloat16


def _dot(a, b):
    return jnp.dot(a, b, preferred_element_type=F32)


def _rms(x):
    return x * lax.rsqrt(jnp.mean(x * x, axis=-1, keepdims=True) + NORM_EPS)


def _sigmoid(x):
    return 1.0 / (1.0 + jnp.exp(-x))


def _in_proj_layout(d):
    return (("ret_q", RET_HEADS, RET_QK_DIM), ("ret_k", RET_HEADS, RET_QK_DIM),
            ("ret_v", RET_HEADS, RET_V_DIM), ("ret_g", RET_HEADS, RET_V_DIM),
            ("moba_q", MOBA_STEPS, LANES), ("moba_k", MOBA_STEPS, LANES), ("moba_v", MOBA_STEPS, LANES),
            ("gate_ret", 1, d), ("gate_moba", 1, d))


def _in_proj_kernel(x_ref, g_ref, w_ref, dq_ref, dk_ref, *out_refs, layout):
    rs = x_ref.shape[0] // ROW_SPLITS
    groups = [slice(i * rs, (i + 1) * rs) for i in range(ROW_SPLITS)]
    hs = [(_rms(x_ref[r, :]) * g_ref[...]).astype(BF16) for r in groups]
    epilogue = {
        "ret_q": lambda z, r, c: z * dq_ref[r, c:c + PROJ_CHUNK],
        "ret_k": lambda z, r, c: z * dk_ref[r, c:c + PROJ_CHUNK],
    }
    off = 0
    for ref, (name, heads, width) in zip(out_refs, layout):
        for c in range(0, heads * width, PROJ_CHUNK):
            for h, r in zip(hs, groups):
                zc = _dot(h, w_ref[:, off + c:off + c + PROJ_CHUNK])
                if name in epilogue:
                    zc = epilogue[name](zc, r, c)
                zc = zc.astype(BF16)
                if heads == 1:
                    ref[r, c:c + PROJ_CHUNK] = zc
                else:
                    for j in range(PROJ_CHUNK // width):
                        ref[0, c // width + j, r, :] = zc[:, j * width:(j + 1) * width]
        off += heads * width


def _retention_decay_tables(rows, chunk):
    log_g = np.log1p(-np.exp2(-5.0 - np.arange(RET_HEADS, dtype=np.float64)))
    steps = (np.arange(rows) % chunk + 1.0)[:, None, None] * log_g[None, :, None]
    expand = lambda a: jnp.asarray(np.broadcast_to(a, (rows, RET_HEADS, RET_QK_DIM)).reshape(rows, RET_QK_W)
                                   .astype(np.float32))
    return expand(np.exp(steps)), expand(np.exp(-steps) * RET_QK_DIM ** -0.5)


def _in_proj(x2d, g, w_bf16, *, b, t, tm):
    m, d = x2d.shape
    layout = _in_proj_layout(d)
    assert t % tm == 0 and tm % RET_KERNEL_CHUNK == 0
    assert w_bf16.shape[1] == sum(h * w for _, h, w in layout)
    tiles = t // tm
    dq, dk = _retention_decay_tables(tm, RET_KERNEL_CHUNK)
    out_specs, out_shapes = [], []
    for _, heads, width in layout:
        assert (heads * width) % PROJ_CHUNK == 0 and PROJ_CHUNK % width == 0 or heads == 1
        if heads == 1:
            out_specs.append(pl.BlockSpec((tm, width), lambda i: (i, 0)))
            out_shapes.append(jax.ShapeDtypeStruct((m, width), BF16))
        else:
            out_specs.append(pl.BlockSpec((1, heads, tm, width), lambda i: (i // tiles, 0, i % tiles, 0)))
            out_shapes.append(jax.ShapeDtypeStruct((b, heads, t, width), BF16))
    const = lambda a: pl.BlockSpec(a.shape, lambda i: (0, 0), pipeline_mode=pl.Buffered(1))
    return pl.pallas_call(
        functools.partial(_in_proj_kernel, layout=layout),
        grid=(m // tm,),
        in_specs=[
            pl.BlockSpec((tm, d), lambda i: (i, 0)),
            pl.BlockSpec((1, d), lambda i: (0, 0)),
            const(w_bf16), const(dq), const(dk),
        ],
        out_specs=out_specs,
        out_shape=out_shapes,
        compiler_params=pltpu.CompilerParams(
            dimension_semantics=("parallel",),
            vmem_limit_bytes=VMEM_LIMIT_BYTES),
        name="in_proj",
    )(x2d, g, w_bf16, dq, dk)


def _retention_kernel(cd_ref, q_ref, k_ref, v_ref, g_ref, o_ref, *, chunk):
    heads, t = q_ref.shape[1], q_ref.shape[2]
    dv = v_ref.shape[3]
    row = lax.broadcasted_iota(jnp.int32, (chunk, chunk), 0)
    col = lax.broadcasted_iota(jnp.int32, (chunk, chunk), 1)
    causal = col <= row
    cds = [cd_ref[h] for h in range(heads)]
    states = [jnp.zeros((q_ref.shape[3], dv), F32) for _ in range(heads)]
    for n in range(t // chunk):
        rows = slice(n * chunk, (n + 1) * chunk)
        for h in range(heads):
            qc = q_ref[0, h, rows, :]
            kc = k_ref[0, h, rows, :]
            vc = v_ref[0, h, rows, :]
            kc_t = kc.T
            scores = jnp.where(causal, _dot(qc, kc_t), 0.0).astype(BF16)
            o = _dot(scores, vc) + _dot(qc, states[h].astype(BF16))
            half_g = 0.5 * g_ref[0, h, rows, :].astype(F32)
            swish = half_g * (1.0 + jnp.tanh(half_g))
            o_ref[0, rows, h * dv:(h + 1) * dv] = (_rms(o) * swish).astype(o_ref.dtype)
            states[h] = cds[h] * (states[h] + _dot(kc_t, vc))


def _retention_pipeline(cd_ref, q_hbm, k_hbm, v_hbm, g_hbm, o_hbm, *, chunk):
    b, heads, t, _ = q_hbm.shape
    head_spec = lambda width: pl.BlockSpec((1, heads, t, width), lambda i: (i, 0, 0, 0),
                                           pipeline_mode=pl.Buffered(RET_INPUT_BUFFERS))
    pltpu.emit_pipeline(
        functools.partial(_retention_kernel, cd_ref, chunk=chunk),
        grid=(b,),
        in_specs=[head_spec(RET_QK_DIM), head_spec(RET_QK_DIM), head_spec(RET_V_DIM), head_spec(RET_V_DIM)],
        out_specs=[pl.BlockSpec((1, t, heads * RET_V_DIM), lambda i: (i, 0, 0))],
    )(q_hbm, k_hbm, v_hbm, g_hbm, o_hbm)


def _retention(q, k, v, g):
    b, heads, t, _ = q.shape
    c = RET_KERNEL_CHUNK
    log_g = np.log1p(-np.exp2(-5.0 - np.arange(RET_HEADS, dtype=np.float64)))
    cd = jnp.asarray(np.exp(log_g * c).astype(np.float32))
    any_spec = pl.BlockSpec(memory_space=pl.ANY)
    return pl.pallas_call(
        functools.partial(_retention_pipeline, chunk=c),
        in_specs=[pl.BlockSpec(memory_space=pltpu.SMEM), any_spec, any_spec, any_spec, any_spec],
        out_specs=any_spec,
        out_shape=jax.ShapeDtypeStruct((b, t, RET_V_W), BF16),
        compiler_params=pltpu.CompilerParams(vmem_limit_bytes=VMEM_LIMIT_BYTES),
        name="retention",
    )(cd, q, k, v, g)


def _moba_key_features(t, blk):
    s = np.arange(t)
    feats = np.zeros((t, LANES), np.float32)
    nb = t // blk
    for f0 in range(0, MOBA_STEP_HEADS * MOBA_HEAD_FEATS, MOBA_HEAD_FEATS):
        feats[s, f0 + s // blk] = 1.0
        feats[:, f0 + nb] = s % blk
        feats[:, f0 + nb + 1] = s // blk
        feats[:, f0 + nb + 2] = 1.0
    return jnp.asarray(feats, dtype=BF16)


def _moba_kernel(slope_ref, q_ref, k_ref, v_ref, kx_ref, o_ref, *maybe_d_ref, blk, topk, head_dim, exact):
    (d_ref,) = maybe_d_ref or (None,)
    t = q_ref.shape[2]
    nb = t // blk
    width = q_ref.shape[3]
    heads_per_step = width // head_dim
    scale = head_dim ** -0.5
    q_all = q_ref[0, 0]
    k_all = k_ref[0, 0]
    k_aug = jnp.concatenate([k_all, kx_ref[...]], axis=1)
    v_t = v_ref[0, 0].astype(F32).T.astype(BF16)
    lane = lax.broadcasted_iota(jnp.int32, (1, width), 1)
    in_head = [(lane >= h * head_dim) & (lane < (h + 1) * head_dim) for h in range(heads_per_step)]

    k_mean = jnp.mean(k_all.astype(F32).reshape(nb, blk, width), axis=1)
    km_hi = k_mean.astype(BF16).astype(F32)
    km_mid = (k_mean - km_hi).astype(BF16).astype(F32)
    km_lo = k_mean - km_hi - km_mid
    pieces = [jnp.where(in_head[h], part, 0.0) for h in range(heads_per_step) for part in (km_hi, km_mid, km_lo)]
    q_t = q_all.astype(F32).T
    gates = _dot(jnp.concatenate(pieces, axis=0).astype(BF16), q_t.astype(BF16))

    key_i = lax.broadcasted_iota(jnp.int32, (blk, blk), 0)
    qry_i = lax.broadcasted_iota(jnp.int32, (blk, blk), 1)
    causal_t = key_i <= qry_i

    g_row = lax.broadcasted_iota(jnp.int32, (nb, t), 0)
    g_qblk = lax.broadcasted_iota(jnp.int32, (nb, t), 1) // blk
    eligible = g_row < g_qblk
    own = g_row == g_qblk
    x_row = lax.broadcasted_iota(jnp.int32, (MOBA_HEAD_FEATS - nb, t), 0)

    q_row = lax.broadcasted_iota(jnp.int32, (width, 1), 0)
    k_f32 = k_all.astype(F32)
    k_sq_max = jnp.max(k_f32 * k_f32, axis=0, keepdims=True)
    t_pos = lax.broadcasted_iota(jnp.int32, (1, t), 1).astype(F32)
    q_scaled_t = q_t * scale
    feat_pad = jnp.zeros((MOBA_HEAD_FEATS, t), F32)
    tail_pad = jnp.zeros((LANES - heads_per_step * MOBA_HEAD_FEATS, t), F32)
    ones_rows = jnp.ones((ONES_ROWS, t), BF16)
    q_augs_t, v_exts = [], []
    for head in range(heads_per_step):
        slope = slope_ref[pl.program_id(1) * heads_per_step + head]
        g0 = 3 * head * nb
        gate = gates[g0:g0 + nb] + gates[g0 + nb:g0 + 2 * nb] + gates[g0 + 2 * nb:g0 + 3 * nb]
        gate = jnp.where(eligible, gate, -jnp.inf)
        rank = jnp.zeros((nb, t), jnp.int32)
        for m in range(nb):
            g_m = gate[m:m + 1, :]
            beats = (g_m > gate) | ((g_m == gate) & (m < g_row))
            rank = rank + beats.astype(jnp.int32)
        selected = eligible & (rank < topk) & (jnp.abs(gate) < jnp.inf)
        sel_bias_t = jnp.where(selected | own, 0.0, MASK_VALUE)
        in_head_rows = (q_row >= head * head_dim) & (q_row < (head + 1) * head_dim)
        q_head = jnp.where(in_head_rows, q_t, 0.0)
        q_sq = jnp.sum(q_head * q_head, axis=0, keepdims=True)
        k_sq = jnp.sum(jnp.where(in_head[head], k_sq_max, 0.0), axis=1, keepdims=True)
        bound = (scale * MOBA_BOUND_SLACK) * jnp.sqrt(q_sq * k_sq)
        shift = -(slope * t_pos + bound)
        slope_rows = jnp.where(x_row == 0, slope, jnp.where(x_row == 1, slope * float(blk),
                                                           jnp.where(x_row == 2, shift, 0.0)))
        parts = [jnp.where(in_head_rows, q_scaled_t, 0.0)]
        parts += [feat_pad] * head + [sel_bias_t, slope_rows] + [feat_pad] * (heads_per_step - 1 - head)
        parts.append(tail_pad)
        q_augs_t.append(jnp.concatenate(parts, axis=0).astype(BF16))
        v_exts.append(jnp.concatenate([v_t[head * head_dim:(head + 1) * head_dim, :], ones_rows], axis=0))

    def scores(qb, head):
        nk = (qb + 1) * blk
        return _dot(k_aug[:nk], q_augs_t[head][:, qb * blk:nk])

    def attend(qb, head, s_t):
        past = qb * blk
        diag = jnp.where(causal_t, s_t[past:], MASK_VALUE)
        m_col = jnp.max(diag, axis=0, keepdims=True)
        if qb:
            m_col = jnp.maximum(m_col, jnp.max(s_t[:past], axis=0, keepdims=True))
        v_e = v_exts[head]
        acc = _dot(v_e[:, past:past + blk], jnp.exp((diag - m_col).astype(BF16)))
        if qb:
            acc = acc + _dot(v_e[:, :past], jnp.exp((s_t[:past] - m_col).astype(BF16)))
        denom = acc[head_dim:head_dim + 1]
        return acc[:head_dim] / denom, denom

    def attend_bounded(qb, head, s_t):
        past = qb * blk
        v_e = v_exts[head]
        acc = _dot(v_e[:, past:past + blk], jnp.exp(jnp.where(causal_t, s_t[past:], MASK_VALUE)).astype(BF16))
        if qb:
            acc = acc + _dot(v_e[:, :past], jnp.exp(s_t[:past]).astype(BF16))
        denom = acc[head_dim:head_dim + 1]
        return acc[:head_dim] / denom, denom

    def run(attend_fn):
        heads = range(heads_per_step)
        order = list(reversed(range(nb)))
        denom_min = None
        pending = [scores(order[0], h) for h in heads]
        for i, qb in enumerate(order):
            nxt = [scores(order[i + 1], h) for h in heads] if i + 1 < nb else None
            block_out_t = []
            for h in heads:
                out_t, denom = attend_fn(qb, h, pending[h])
                denom_min = denom if denom_min is None else jnp.minimum(denom_min, denom)
                block_out_t.append(out_t)
            o_ref[0, qb * blk:(qb + 1) * blk, :] = jnp.concatenate(block_out_t, axis=0).T.astype(o_ref.dtype)
            pending = nxt
        return denom_min

    if exact:
        run(attend)
    else:
        denom_min = run(attend_bounded)
        d_ref[0, 0] = jnp.broadcast_to(jnp.min(denom_min, axis=1, keepdims=True), d_ref.shape[2:])


def _moba(q, k, v, *, exact):
    b, steps, t, width = q.shape
    assert t % MOBA_BLOCK == 0 and t // MOBA_BLOCK + 3 <= MOBA_HEAD_FEATS
    slopes = jnp.asarray(np.exp2(-8.0 / MOBA_HEADS * np.arange(1, MOBA_HEADS + 1)).astype(np.float32))
    spec = pl.BlockSpec((1, 1, t, width), lambda i, h: (i, h, 0, 0))
    out_specs = [pl.BlockSpec((1, t, width), lambda i, h: (i, 0, h))]
    out_shapes = [jax.ShapeDtypeStruct((b, t, MOBA_W), BF16)]
    if not exact:
        out_specs.append(pl.BlockSpec((1, 1, SUBLANES, LANES), lambda i, h: (i, h, 0, 0)))
        out_shapes.append(jax.ShapeDtypeStruct((b, steps, SUBLANES, LANES), F32))
    outs = pl.pallas_call(
        functools.partial(_moba_kernel, blk=MOBA_BLOCK, topk=MOBA_TOPK, head_dim=MOBA_HEAD_DIM, exact=exact),
        grid=(b, steps),
        in_specs=[pl.BlockSpec(memory_space=pltpu.SMEM), spec, spec, spec,
                  pl.BlockSpec((t, LANES), lambda i, h: (0, 0))],
        out_specs=out_specs,
        out_shape=out_shapes,
        compiler_params=pltpu.CompilerParams(
            dimension_semantics=("parallel", "arbitrary"),
            vmem_limit_bytes=VMEM_LIMIT_BYTES),
        name="moba_exact" if exact else "moba",
    )(slopes, q, k, v, _moba_key_features(t, MOBA_BLOCK))
    return outs[0] if exact else tuple(outs)


def _moba_guarded(q, k, v):
    out, denoms = _moba(q, k, v, exact=False)
    return lax.cond(jnp.min(denoms) > MOBA_DENOM_FLOOR, lambda: out, lambda: _moba(q, k, v, exact=True))


def _post_kernel(x_ref, ret_ref, mo_ref, gr_ref, gm_ref, bgr_ref, bgm_ref, wro_ref, wmo_ref,
                 wout_ref, g2_ref, wup_ref, wdn_ref, gf_ref, o_ref, *, ff_chunk):
    rs = x_ref.shape[0] // ROW_SPLITS
    groups = [slice(i * rs, (i + 1) * rs) for i in range(ROW_SPLITS)]
    y_ret = [_dot(ret_ref[r, :], wro_ref[...]) for r in groups]
    y_moba = [_dot(mo_ref[r, :], wmo_ref[...]) for r in groups]
    merged = []
    for i, r in enumerate(groups):
        gate_r = _sigmoid(gr_ref[r, :].astype(F32) + bgr_ref[...])
        gate_m = _sigmoid(gm_ref[r, :].astype(F32) + bgm_ref[...])
        merged.append((gate_r * y_ret[i] + gate_m * y_moba[i]).astype(BF16))
    x1 = [x_ref[r, :] + _dot(merged[i], wout_ref[...]) for i, r in enumerate(groups)]
    h2 = [(_rms(x) * g2_ref[...]).astype(BF16) for x in x1]
    acc = list(x1)
    for c in range(0, wup_ref.shape[1], ff_chunk):
        u = [_dot(h, wup_ref[:, c:c + ff_chunk]) for h in h2]
        u = [jnp.square(jnp.maximum(v, 0.0)).astype(BF16) for v in u]
        acc = [a + _dot(v, wdn_ref[c:c + ff_chunk, :]) for a, v in zip(acc, u)]
    for a, r in zip(acc, groups):
        o_ref[r, :] = (_rms(a) * gf_ref[...]).astype(o_ref.dtype)


def _post(x2d, ret2d, mo2d, gr2d, gm2d, bg_r, bg_m, w_ret_o, w_moba_o, w_out, g2, w_up, w_down, gf,
          *, tm):
    m, d = x2d.shape
    dff = w_up.shape[1]
    row = lambda width: pl.BlockSpec((tm, width), lambda i: (i, 0))
    const = lambda shape: pl.BlockSpec(shape, lambda i: (0, 0), pipeline_mode=pl.Buffered(1))
    return pl.pallas_call(
        functools.partial(_post_kernel, ff_chunk=MLP_HIDDEN_CHUNK),
        grid=(m // tm,),
        in_specs=[
            row(d), row(RET_V_W), row(MOBA_W), row(d), row(d),
            const((1, d)), const((1, d)),
            const((RET_V_W, d)), const((MOBA_W, d)), const((d, d)),
            const((1, d)), const((d, dff)), const((dff, d)), const((1, d)),
        ],
        out_specs=row(d),
        out_shape=jax.ShapeDtypeStruct((m, d), x2d.dtype),
        compiler_params=pltpu.CompilerParams(
            dimension_semantics=("parallel",),
            vmem_limit_bytes=VMEM_LIMIT_BYTES),
        name="merge_mlp",
    )(x2d, ret2d, mo2d, gr2d, gm2d, bg_r, bg_m, w_ret_o, w_moba_o, w_out, g2, w_up, w_down, gf)


def kernel(x, norm1_g, w_in, b_gate, w_ret_o, w_moba_o, w_out, norm2_g, w_up, w_down, normf_g):
    b, t, d = x.shape
    assert norm1_g.shape[0] == 1, "single-layer block"
    m = b * t
    x2d = x.reshape(m, d)
    rq, rk, rv, rg, mq, mk, mv, gr, gm = _in_proj(x2d, norm1_g, w_in[0].astype(BF16), b=b, t=t,
                                                  tm=IN_PROJ_ROWS)
    ret = _retention(rq, rk, rv, rg)
    mo = _moba_guarded(mq, mk, mv)
    bg = b_gate[0].reshape(2, 1, d)
    out = _post(x2d, ret.reshape(m, RET_V_W), mo.reshape(m, MOBA_W), gr, gm, bg[0], bg[1],
                w_ret_o[0].astype(BF16), w_moba_o[0].astype(BF16), w_out[0].astype(BF16),
                norm2_g, w_up[0].astype(BF16), w_down[0].astype(BF16), normf_g.reshape(1, d),
                tm=MERGE_ROWS)
    return out.reshape(b, t, d)
```
